```python
import jax, jax.numpy as jnp
from jax import lax
import numpy as np

D_MODEL = 2048
BATCH = 1
SEQ = 16384
DEPTH = 4
DEC_BATCH = 16
DEC_SEQ = 32
PAST_LEN = 1024

CHUNK = 64
Q_BLOCK = 128
GLA_HEADS = 4
GLA_DK = D_MODEL // 2
GLA_DV = D_MODEL
GLA_HEAD_K = GLA_DK // GLA_HEADS
GLA_HEAD_V = GLA_DV // GLA_HEADS
GLA_LOWRANK = 16
GLA_GATE_NORMALIZER = 16.0
FOX_HEADS = 32
FOX_HEAD_DIM = 64
FOX_DIM = FOX_HEADS * FOX_HEAD_DIM
D_FF = 4 * D_MODEL
EPS = 1e-6

SPLIT_SIZES = (GLA_DK, GLA_DK, GLA_DV, GLA_DV, GLA_LOWRANK, FOX_DIM, FOX_DIM, FOX_DIM, FOX_HEADS, D_MODEL, D_MODEL)
N_IN = GLA_DK * 2 + GLA_DV * 2 + GLA_LOWRANK + FOX_DIM * 3 + FOX_HEADS + D_MODEL * 2

kernel_name = 'gla_fox_parallel_streaming_encoder'


def rmsnorm(x, g):
    x32 = x.astype(jnp.float32)
    y = x32 * lax.rsqrt(jnp.mean(x32 * x32, axis=-1, keepdims=True) + EPS)
    return (y * g.astype(jnp.float32)).astype(x.dtype)


def split_columns(proj):
    pieces = []
    start = 0
    for size in SPLIT_SIZES:
        pieces.append(proj[..., start:start + size])
        start += size
    return pieces


def gla_recurrence(q, k, v, log_a, s0):
    B, L, H, HK = q.shape
    HV = v.shape[-1]
    c = min(CHUNK, L)
    n = L // c

    def blocks(t):
        t = t.astype(jnp.float32)
        return jnp.moveaxis(t.reshape((B, n, c) + t.shape[2:]), 1, 0)

    qb = blocks(q) * (GLA_HEAD_K ** -0.5)
    kb = blocks(k)
    vb = blocks(v)
    bb = jnp.cumsum(blocks(log_a), axis=2)
    causal = jnp.tril(jnp.ones((c, c), dtype=bool))[None, :, :, None, None]

    def step(S, inp):
        qc, kc, vc, bc = inp
        o_inter = jnp.einsum('bthk,bhkv->bthv', qc * jnp.exp(bc), S)
        diff = bc[:, :, None] - bc[:, None, :]
        decay = jnp.exp(jnp.where(causal, diff, -jnp.inf))
        scores = jnp.einsum('bthk,bshk,btshk->bhts', qc, kc, decay)
        o_intra = jnp.einsum('bhts,bshv->bthv', scores, vc)
        b_last = bc[:, -1]
        k_dec = kc * jnp.exp(b_last[:, None] - bc)
        S_new = S * jnp.exp(b_last)[..., None] + jnp.einsum('bshk,bshv->bhkv', k_dec, vc)
        return S_new, o_inter + o_intra

    s_fin, ob = lax.scan(step, s0.astype(jnp.float32), (qb, kb, vb, bb))
    o = jnp.moveaxis(ob, 0, 1).reshape(B, L, H, HV)
    return o, s_fin


def fox_attend(q, k, v, cq, ck, q_pos, k_pos):
    logits = jnp.einsum('bqhd,bkhd->bhqk', q.astype(jnp.float32), k.astype(jnp.float32)) * (FOX_HEAD_DIM ** -0.5)
    bias = jnp.swapaxes(cq, 1, 2)[..., :, None] - jnp.swapaxes(ck, 1, 2)[..., None, :]
    mask = k_pos[None, :] <= q_pos[:, None]
    logits = jnp.where(mask, logits + bias, -jnp.inf)
    p = jax.nn.softmax(logits, axis=-1)
    return jnp.einsum('bhqk,bkhd->bqhd', p, v.astype(jnp.float32)).astype(v.dtype)


def fox_mixer(q, k, v, logf, cache_k, cache_v, cache_logf):
    B, L, H, D = q.shape
    if cache_k is None:
        kk, vv, ff, past = k, v, logf, 0
    else:
        kk = jnp.concatenate([cache_k.astype(k.dtype), k], axis=1)
        vv = jnp.concatenate([cache_v.astype(v.dtype), v], axis=1)
        ff = jnp.concatenate([cache_logf.astype(jnp.float32), logf.astype(jnp.float32)], axis=1)
        past = cache_k.shape[1]
    c = jnp.cumsum(ff.astype(jnp.float32), axis=1)
    cq = c[:, past:]
    k_pos = jnp.arange(past + L)
    q_pos = past + jnp.arange(L)
    if L % Q_BLOCK == 0:
        nb = L // Q_BLOCK
        qb = jnp.moveaxis(q.reshape(B, nb, Q_BLOCK, H, D), 1, 0)
        cqb = jnp.moveaxis(cq.reshape(B, nb, Q_BLOCK, H), 1, 0)
        pb = q_pos.reshape(nb, Q_BLOCK)
        out = lax.map(lambda blk: fox_attend(blk[0], kk, vv, blk[1], c, blk[2], k_pos), (qb, cqb, pb))
        return jnp.moveaxis(out, 0, 1).reshape(B, L, H, D)
    return fox_attend(q, kk, vv, cq, c, q_pos, k_pos)


def encoder_layer(x, cache_k, cache_v, cache_logf, gla_s0, norm_mix_g, w_in, gla_alpha_up, gla_alpha_b,
                  gla_norm_g, fox_f_b, fox_qnorm_g, fox_knorm_g, w_br_gla, w_br_fox, w_out,
                  norm_mlp_g, w_up, w_down):
    B, L, _ = x.shape
    h = rmsnorm(x, norm_mix_g)
    proj = h @ w_in
    gq, gk, gv, gg, g_low, fq, fk, fv, f_logit, gate_a, gate_b = split_columns(proj)

    a_logit = (g_low @ gla_alpha_up + gla_alpha_b).astype(jnp.float32)
    log_a = jax.nn.log_sigmoid(a_logit) / GLA_GATE_NORMALIZER
    if gla_s0 is None:
        gla_s0 = jnp.zeros((B, GLA_HEADS, GLA_HEAD_K, GLA_HEAD_V), jnp.float32)
    o_gla, s_new = gla_recurrence(gq.reshape(B, L, GLA_HEADS, GLA_HEAD_K), gk.reshape(B, L, GLA_HEADS, GLA_HEAD_K),
                                  gv.reshape(B, L, GLA_HEADS, GLA_HEAD_V), log_a.reshape(B, L, GLA_HEADS, GLA_HEAD_K),
                                  gla_s0)
    o_gla = rmsnorm(o_gla, gla_norm_g) * jax.nn.silu(gg.reshape(B, L, GLA_HEADS, GLA_HEAD_V).astype(jnp.float32))
    br_a = o_gla.reshape(B, L, GLA_DV).astype(x.dtype) @ w_br_gla

    q = rmsnorm(fq.reshape(B, L, FOX_HEADS, FOX_HEAD_DIM), fox_qnorm_g)
    k = rmsnorm(fk.reshape(B, L, FOX_HEADS, FOX_HEAD_DIM), fox_knorm_g)
    v = fv.reshape(B, L, FOX_HEADS, FOX_HEAD_DIM)
    logf = jax.nn.log_sigmoid(f_logit.astype(jnp.float32) + fox_f_b.astype(jnp.float32))
    o_fox = fox_mixer(q, k, v, logf, cache_k, cache_v, cache_logf)
    br_b = o_fox.reshape(B, L, FOX_DIM).astype(x.dtype) @ w_br_fox

    merged = jax.nn.sigmoid(gate_a) * br_a + jax.nn.sigmoid(gate_b) * br_b
    x = x + merged @ w_out

    h2 = rmsnorm(x, norm_mlp_g)
    u = jax.nn.relu(h2 @ w_up)
    x = x + (u * u) @ w_down
    return x, k, v, logf, s_new


def setup_inputs(seed: int = 0) -> dict:
    key = jax.random.key(seed)
    ks = jax.random.split(key, 24)

    def nrm(k, shape, scale):
        return jax.random.normal(k, shape, jnp.float32) * scale

    return {
        'x_prompt': nrm(ks[0], (BATCH, SEQ, D_MODEL), 1.0),
        'x_sample': nrm(ks[1], (DEC_BATCH, DEC_SEQ, D_MODEL), 1.0),
        'cache_fox_k': nrm(ks[2], (DEPTH, DEC_BATCH, PAST_LEN, FOX_HEADS, FOX_HEAD_DIM), 1.0),
        'cache_fox_v': nrm(ks[3], (DEPTH, DEC_BATCH, PAST_LEN, FOX_HEADS, FOX_HEAD_DIM), 1.0),
        'cache_fox_logf': jax.nn.log_sigmoid(3.0 + nrm(ks[4], (DEPTH, DEC_BATCH, PAST_LEN, FOX_HEADS), 1.0)),
        'state_gla': nrm(ks[5], (DEPTH, DEC_BATCH, GLA_HEADS, GLA_HEAD_K, GLA_HEAD_V), 0.5),
        'norm_mix_g': 1.0 + nrm(ks[6], (DEPTH, D_MODEL), 0.02),
        'w_in': nrm(ks[7], (DEPTH, D_MODEL, N_IN), D_MODEL ** -0.5),
        'gla_alpha_up': nrm(ks[8], (DEPTH, GLA_LOWRANK, GLA_DK), GLA_LOWRANK ** -0.5),
        'gla_alpha_b': nrm(ks[9], (DEPTH, GLA_DK), 0.1),
        'gla_norm_g': 1.0 + nrm(ks[10], (DEPTH, GLA_HEAD_V), 0.02),
        'fox_f_b': jax.random.uniform(ks[11], (DEPTH, FOX_HEADS), jnp.float32, 1.0, 6.0),
        'fox_qnorm_g': 1.0 + nrm(ks[12], (DEPTH, FOX_HEAD_DIM), 0.02),
        'fox_knorm_g': 1.0 + nrm(ks[13], (DEPTH, FOX_HEAD_DIM), 0.02),
        'w_br_gla': nrm(ks[14], (DEPTH, GLA_DV, D_MODEL), GLA_DV ** -0.5),
        'w_br_fox': nrm(ks[15], (DEPTH, FOX_DIM, D_MODEL), FOX_DIM ** -0.5),
        'w_out': nrm(ks[16], (DEPTH, D_MODEL, D_MODEL), D_MODEL ** -0.5),
        'norm_mlp_g': 1.0 + nrm(ks[17], (DEPTH, D_MODEL), 0.02),
        'w_up': nrm(ks[18], (DEPTH, D_MODEL, D_FF), D_MODEL ** -0.5),
        'w_down': nrm(ks[19], (DEPTH, D_FF, D_MODEL), D_FF ** -0.5),
    }


def reference(x_prompt, x_sample, cache_fox_k, cache_fox_v, cache_fox_logf, state_gla, norm_mix_g, w_in,
              gla_alpha_up, gla_alpha_b, gla_norm_g, fox_f_b, fox_qnorm_g, fox_knorm_g, w_br_gla, w_br_fox,
              w_out, norm_mlp_g, w_up, w_down):
    xp, xs = x_prompt, x_sample
    kp, vp, fp, sp = [], [], [], []
    ksm, vsm, fsm, ssm = [], [], [], []
    for l in range(DEPTH):
        weights = (norm_mix_g[l], w_in[l], gla_alpha_up[l], gla_alpha_b[l], gla_norm_g[l], fox_f_b[l],
                   fox_qnorm_g[l], fox_knorm_g[l], w_br_gla[l], w_br_fox[l], w_out[l], norm_mlp_g[l],
                   w_up[l], w_down[l])
        xp, k1, v1, f1, s1 = encoder_layer(xp, None, None, None, None, *weights)
        xs, k2, v2, f2, s2 = encoder_layer(xs, cache_fox_k[l], cache_fox_v[l], cache_fox_logf[l], state_gla[l], *weights)
        kp.append(k1); vp.append(v1); fp.append(f1); sp.append(s1)
        ksm.append(k2); vsm.append(v2); fsm.append(f2); ssm.append(s2)
    fox_k_prompt = jnp.stack(kp)
    fox_v_prompt = jnp.stack(vp)
    fox_logf_prompt = jnp.stack(fp)
    gla_state_prompt = jnp.stack(sp)
    fox_k_sample = jnp.stack(ksm)
    fox_v_sample = jnp.stack(vsm)
    fox_logf_sample = jnp.stack(fsm)
    gla_state_sample = jnp.stack(ssm)
    return (xp, xs, fox_k_prompt, fox_v_prompt, fox_logf_prompt, gla_state_prompt,
            fox_k_sample, fox_v_sample, fox_logf_sample, gla_state_sample)
```

```python
import functools

import jax
import jax.numpy as jnp
from jax import lax
from jax.experimental import pallas as pl
from jax.experimental.pallas import tpu as pltpu

F32 = jnp.float32
BF16 = jnp.bfloat16

D_MODEL = 2048
DEPTH = 4
GLA_HEADS = 4
GLA_HEAD_K = 256
GLA_HEAD_V = 512
GLA_DK = GLA_HEADS * GLA_HEAD_K
GLA_DV = GLA_HEADS * GLA_HEAD_V
GLA_LOWRANK = 16
GLA_GATE_NORMALIZER = 16.0
GLA_CHUNK = 64
GLA_SUB = 16
FOX_HEADS = 32
FOX_HEAD_DIM = 64
FOX_DIM = FOX_HEADS * FOX_HEAD_DIM
D_FF = 4 * D_MODEL
EPS = 1e-6

LANES = 128
MXU_DIM = 256
VMEM_LIMIT_BYTES = 56 * 1024 * 1024

CE_PARTS = 3
CE_ONES = CE_PARTS * FOX_HEADS
LOWRANK_LANE = CE_ONES

NT_DIMS = (((1,), (1,)), ((), ()))
TN_DIMS = (((0,), (0,)), ((), ()))


def _params(*semantics):
    return pltpu.CompilerParams(dimension_semantics=semantics, vmem_limit_bytes=VMEM_LIMIT_BYTES)


def _log_sigmoid(x):
    return jnp.minimum(x, 0.0) - jnp.log(1.0 + jnp.exp(-jnp.abs(x)))


def _split3(x):
    p1 = x.astype(BF16)
    r1 = x - p1.astype(F32)
    p2 = r1.astype(BF16)
    p3 = (r1 - p2.astype(F32)).astype(BF16)
    return p1, p2, p3


def _lower_tri(n):
    r = lax.broadcasted_iota(jnp.int32, (n, n), 0)
    c = lax.broadcasted_iota(jnp.int32, (n, n), 1)
    return jnp.where(r >= c, 1.0, 0.0).astype(BF16)


def _cumsum_rows(tri, x):
    p1, p2, p3 = _split3(x)
    return (jnp.dot(tri, p1, preferred_element_type=F32)
            + jnp.dot(tri, p2, preferred_element_type=F32)
            + jnp.dot(tri, p3, preferred_element_type=F32))


def _rmsnorm_to_bf16(x_ref, g_ref, h_ref, rows=128):
    def body(r, carry):
        sl = pl.ds(pl.multiple_of(r * rows, rows), rows)
        x = x_ref[sl, :]
        ms = jnp.mean(x * x, axis=-1, keepdims=True)
        h_ref[sl, :] = (x * lax.rsqrt(ms + EPS) * g_ref[...]).astype(BF16)
        return carry
    lax.fori_loop(0, x_ref.shape[0] // rows, body, 0)


def _head_rms(acc, gsum_ref, gain):
    sq = acc * acc
    hi = sq.astype(BF16)
    lo = (sq - hi.astype(F32)).astype(BF16)
    parts = []
    for s in range(acc.shape[1] // MXU_DIM):
        sl = slice(s * MXU_DIM, (s + 1) * MXU_DIM)
        parts.append(jnp.dot(hi[:, sl], gsum_ref[...], preferred_element_type=F32)
                     + jnp.dot(lo[:, sl], gsum_ref[...], preferred_element_type=F32))
    ms = jnp.concatenate(parts, axis=1) * (1.0 / FOX_HEAD_DIM)
    return acc * lax.rsqrt(ms + EPS) * gain


def _proj_kernel(x_ref, g_ref, w_ref, *refs, mode):
    h_ref = refs[-1]

    @pl.when(pl.program_id(1) == 0)
    def _():
        _rmsnorm_to_bf16(x_ref, g_ref, h_ref)

    acc = jnp.dot(h_ref[...], w_ref[...], preferred_element_type=F32)
    if mode == "plain":
        (o_ref,) = refs[:-1]
        o_ref[...] = acc.astype(o_ref.dtype)
    elif mode == "small":
        b_ref, o_ref = refs[:-1]
        lane = lax.broadcasted_iota(jnp.int32, acc.shape, 1)
        o_ref[...] = jnp.where(lane < CE_ONES, _log_sigmoid(acc + b_ref[...]), acc)
    elif mode == "qnorm":
        gsum_ref, gain_ref, o_ref = refs[:-1]
        o_ref[...] = (_head_rms(acc, gsum_ref, gain_ref[...]) * (FOX_HEAD_DIM ** -0.5)).astype(BF16)
    elif mode == "knorm":
        gsum_ref, gain_ref, o32_ref, o16_ref = refs[:-1]
        y = _head_rms(acc, gsum_ref, gain_ref[...])
        o32_ref[...] = y
        o16_ref[...] = y.astype(BF16)
    else:
        o32_ref, o16_ref = refs[:-1]
        o32_ref[...] = acc
        o16_ref[...] = acc.astype(BF16)


def _proj(x, g, w, mode, extras=()):
    m, n = x.shape[0], w.shape[1]
    tm = min(m, 1024)
    tn = min(n, 512)
    grid = (m // tm, n // tn)
    in_specs = [
        pl.BlockSpec((tm, D_MODEL), lambda i, j: (i, 0)),
        pl.BlockSpec((1, D_MODEL), lambda i, j: (0, 0)),
        pl.BlockSpec((D_MODEL, tn), lambda i, j: (0, j)),
    ]
    tile = pl.BlockSpec((tm, tn), lambda i, j: (i, j))
    row = pl.BlockSpec((1, tn), lambda i, j: (0, j))
    if mode == "plain":
        out_shape, out_specs = jax.ShapeDtypeStruct((m, n), BF16), tile
    elif mode == "small":
        in_specs += [row]
        out_shape, out_specs = jax.ShapeDtypeStruct((m, n), F32), tile
    elif mode == "qnorm":
        in_specs += [pl.BlockSpec((MXU_DIM, MXU_DIM), lambda i, j: (0, 0)), row]
        out_shape, out_specs = jax.ShapeDtypeStruct((m, n), BF16), tile
    else:
        if mode == "knorm":
            in_specs += [pl.BlockSpec((MXU_DIM, MXU_DIM), lambda i, j: (0, 0)), row]
        out_shape = (jax.ShapeDtypeStruct((m, n), F32), jax.ShapeDtypeStruct((m, n), BF16))
        out_specs = (tile, tile)
    return pl.pallas_call(
        functools.partial(_proj_kernel, mode=mode),
        grid=grid, in_specs=in_specs, out_specs=out_specs, out_shape=out_shape,
        scratch_shapes=[pltpu.VMEM((tm, D_MODEL), BF16)],
        compiler_params=_params("parallel", "arbitrary"),
        name="proj_" + mode,
    )(x, g, w, *extras)


def _encode_cumsum(cs):
    c1, c2, c3 = _split3(cs)
    lane = lax.broadcasted_iota(jnp.int32, cs.shape, 1)
    one = jnp.ones_like(c1)
    return jnp.where(lane < FOX_HEADS, c1,
                     jnp.where(lane < 2 * FOX_HEADS, c2, jnp.where(lane < CE_ONES, c3, one)))


def _ce_prompt_kernel(s_ref, ce_ref, carry_ref):
    @pl.when(pl.program_id(0) == 0)
    def _():
        carry_ref[...] = jnp.zeros_like(carry_ref)

    t = s_ref.shape[0]
    cs = _cumsum_rows(_lower_tri(t), s_ref[...]) + carry_ref[...]
    carry_ref[...] = cs[t - 1:t, :]
    ce_ref[...] = _encode_cumsum(cs)


def _ce_prompt(small):
    m = small.shape[0]
    t = MXU_DIM
    return pl.pallas_call(
        _ce_prompt_kernel,
        grid=(m // t,),
        in_specs=[pl.BlockSpec((t, LANES), lambda i: (i, 0))],
        out_specs=pl.BlockSpec((t, LANES), lambda i: (i, 0)),
        out_shape=jax.ShapeDtypeStruct((m, LANES), BF16),
        scratch_shapes=[pltpu.VMEM((1, LANES), F32)],
        compiler_params=_params("arbitrary"),
        name="ce_prompt",
    )(small)


def _ce_sample_kernel(c_ref, s_ref, cec_ref, cen_ref):
    t = MXU_DIM
    tri = _lower_tri(t)
    carry = jnp.zeros((1, LANES), F32)
    for b in range(c_ref.shape[1] // t):
        cs = _cumsum_rows(tri, c_ref[0, b * t:(b + 1) * t, :]) + carry
        carry = cs[t - 1:t, :]
        cec_ref[0, b * t:(b + 1) * t, :] = _encode_cumsum(cs)
    n = s_ref.shape[0]
    cs = _cumsum_rows(_lower_tri(n), s_ref[...]) + carry
    cen_ref[...] = _encode_cumsum(cs)


def _ce_sample(cache_rep, small, batch, seq):
    past = cache_rep.shape[1]
    return pl.pallas_call(
        _ce_sample_kernel,
        grid=(batch,),
        in_specs=[pl.BlockSpec((1, past, LANES), lambda b: (b, 0, 0)),
                  pl.BlockSpec((seq, LANES), lambda b: (b, 0))],
        out_specs=(pl.BlockSpec((1, past, LANES), lambda b: (b, 0, 0)),
                   pl.BlockSpec((seq, LANES), lambda b: (b, 0))),
        out_shape=(jax.ShapeDtypeStruct((batch, past, LANES), BF16),
                   jax.ShapeDtypeStruct((batch * seq, LANES), BF16)),
        compiler_params=_params("parallel"),
        name="ce_sample",
    )(cache_rep, small)


def _gla_kernel(*refs, chunk, has_state):
    if has_state:
        (q_ref, k_ref, v_ref, gg_ref, sm_ref, au_ref, ab_ref, gn_ref, s0_ref,
         o_ref, sout_ref, st_ref) = refs
    else:
        (q_ref, k_ref, v_ref, gg_ref, sm_ref, au_ref, ab_ref, gn_ref,
         o_ref, sout_ref, st_ref) = refs
    blk = pl.program_id(2)

    @pl.when(blk == 0)
    def _():
        if has_state:
            st_ref[...] = s0_ref[0, 0].T
        else:
            st_ref[...] = jnp.zeros_like(st_ref)

    scale = GLA_HEAD_K ** -0.5
    tri = _lower_tri(chunk)
    for c in range(q_ref.shape[0] // chunk):
        rows = slice(c * chunk, (c + 1) * chunk)
        a = jnp.dot(sm_ref[rows, :].astype(BF16), au_ref[...], preferred_element_type=F32) + ab_ref[...]
        log_a = _log_sigmoid(a) * (1.0 / GLA_GATE_NORMALIZER)
        bc = _cumsum_rows(tri, log_a)
        b_last = bc[chunk - 1:chunk, :]
        q = q_ref[rows, :].astype(F32)
        k = k_ref[rows, :].astype(F32)
        v = v_ref[rows, :]
        st = st_ref[...]

        q_inter = (q * jnp.exp(bc) * scale).astype(BF16)
        o = lax.dot_general(q_inter, st.astype(BF16), NT_DIMS, preferred_element_type=F32)

        parts = []
        for i in range(chunk // GLA_SUB):
            r0, r1 = i * GLA_SUB, (i + 1) * GLA_SUB
            anchor = bc[r0:r0 + 1, :]
            q_t = (q[r0:r1] * jnp.exp(bc[r0:r1] - anchor) * scale).astype(BF16)
            k_t = (k[:r1] * jnp.exp(anchor - bc[:r1])).astype(BF16)
            att = lax.dot_general(q_t, k_t, NT_DIMS, preferred_element_type=F32)
            ri = lax.broadcasted_iota(jnp.int32, att.shape, 0) + r0
            ci = lax.broadcasted_iota(jnp.int32, att.shape, 1)
            att = jnp.where(ci <= ri, att, 0.0)
            parts.append(jnp.dot(att.astype(BF16), v[:r1], preferred_element_type=F32))
        o = o + jnp.concatenate(parts, axis=0)

        k_dec = (k * jnp.exp(b_last - bc)).astype(BF16)
        st_ref[...] = st * jnp.exp(b_last) + lax.dot_general(v, k_dec, TN_DIMS, preferred_element_type=F32)

        ms = jnp.mean(o * o, axis=-1, keepdims=True)
        y = o * lax.rsqrt(ms + EPS) * gn_ref[...]
        gg = gg_ref[rows, :].astype(F32)
        o_ref[rows, :] = (y * (gg * jax.nn.sigmoid(gg))).astype(BF16)

    @pl.when(blk == pl.num_programs(2) - 1)
    def _():
        sout_ref[0, 0] = st_ref[...].T


def _gla(pg, small, alpha_pad, alpha_b, gnorm, s0, batch, seq):
    chunk = min(GLA_CHUNK, seq)
    tb = min(seq, 4 * chunk)
    nblk = seq // tb
    hk, hv = GLA_HEAD_K, GLA_HEAD_V
    row = lambda b, h, t: b * nblk + t
    in_specs = [
        pl.BlockSpec((tb, hk), lambda b, h, t: (row(b, h, t), h)),
        pl.BlockSpec((tb, hk), lambda b, h, t: (row(b, h, t), GLA_HEADS + h)),
        pl.BlockSpec((tb, hv), lambda b, h, t: (row(b, h, t), GLA_HEADS + h)),
        pl.BlockSpec((tb, hv), lambda b, h, t: (row(b, h, t), 2 * GLA_HEADS + h)),
        pl.BlockSpec((tb, LANES), lambda b, h, t: (row(b, h, t), 0)),
        pl.BlockSpec((LANES, hk), lambda b, h, t: (0, h)),
        pl.BlockSpec((1, hk), lambda b, h, t: (0, h)),
        pl.BlockSpec((1, hv), lambda b, h, t: (0, 0)),
    ]
    args = [pg, pg, pg, pg, small, alpha_pad, alpha_b, gnorm]
    state_spec = pl.BlockSpec((1, 1, hk, hv), lambda b, h, t: (b, h, 0, 0))
    if s0 is not None:
        in_specs.append(state_spec)
        args.append(s0)
    return pl.pallas_call(
        functools.partial(_gla_kernel, chunk=chunk, has_state=s0 is not None),
        grid=(batch, GLA_HEADS, nblk),
        in_specs=in_specs,
        out_specs=(pl.BlockSpec((tb, hv), lambda b, h, t: (row(b, h, t), h)), state_spec),
        out_shape=(jax.ShapeDtypeStruct((batch * seq, GLA_DV), BF16),
                   jax.ShapeDtypeStruct((batch, GLA_HEADS, hk, hv), F32)),
        scratch_shapes=[pltpu.VMEM((hv, hk), F32)],
        compiler_params=_params("parallel", "parallel", "arbitrary"),
        name="gla",
    )(*args)


def _fox_query_operand(q, ce_q, head, half):
    lane = lax.broadcasted_iota(jnp.int32, (1, LANES), 1)
    keep = jnp.where((lane >= half * FOX_HEAD_DIM) & (lane < (half + 1) * FOX_HEAD_DIM), 1.0, 0.0)
    q_m = q * keep.astype(BF16)
    r = lax.broadcasted_iota(jnp.int32, (LANES, LANES), 0)
    c = lax.broadcasted_iota(jnp.int32, (LANES, LANES), 1)
    pick = jnp.zeros((LANES, LANES), F32)
    for part in range(CE_PARTS):
        pick = jnp.where((r == head + part * FOX_HEADS) & (c == CE_ONES + part), 1.0, pick)
    enc = jnp.dot(ce_q, pick.astype(BF16), preferred_element_type=F32)
    minus = jnp.zeros((1, LANES), F32)
    for part in range(CE_PARTS):
        minus = jnp.where(lane == head + part * FOX_HEADS, -1.0, minus)
    return jnp.concatenate([q_m, (enc + minus).astype(BF16)], axis=1)


def _fox_prompt_kernel(q_ref, ceq_ref, k_ref, cek_ref, vt_ref, o_ref, *, tile):
    pair = pl.program_id(0)
    qi = pl.program_id(1)
    outs = []
    for half in range(2):
        q_op = _fox_query_operand(q_ref[...], ceq_ref[...], 2 * pair + half, half)

        def block(j, carry, masked):
            m, l, acc = carry
            ks = pl.multiple_of(j * tile, tile)
            k_op = jnp.concatenate([k_ref[pl.ds(ks, tile), :], cek_ref[pl.ds(ks, tile), :]], axis=1)
            st = lax.dot_general(k_op, q_op, NT_DIMS, preferred_element_type=F32)
            if masked:
                kr = lax.broadcasted_iota(jnp.int32, st.shape, 0)
                qc = lax.broadcasted_iota(jnp.int32, st.shape, 1)
                st = jnp.where(kr <= qc, st, -jnp.inf)
            m_new = jnp.maximum(m, jnp.max(st, axis=0, keepdims=True))
            alpha = jnp.exp(m - m_new)
            p = jnp.exp(st - m_new)
            l = alpha * l + jnp.sum(p, axis=0, keepdims=True)
            vt = vt_ref[half * FOX_HEAD_DIM:(half + 1) * FOX_HEAD_DIM, pl.ds(ks, tile)]
            acc = alpha * acc + jnp.dot(vt, p.astype(BF16), preferred_element_type=F32)
            return m_new, l, acc

        carry = (jnp.full((1, tile), -jnp.inf, F32), jnp.zeros((1, tile), F32),
                 jnp.zeros((FOX_HEAD_DIM, tile), F32))
        carry = lax.fori_loop(0, qi, functools.partial(block, masked=False), carry)
        _, l, acc = block(qi, carry, True)
        outs.append(acc / l)
    o_ref[...] = jnp.concatenate(outs, axis=0).T.astype(BF16)


def _fox_prompt(q16, k16, vt16, ce):
    m = q16.shape[0]
    tile = MXU_DIM
    pairs = FOX_HEADS // 2
    return pl.pallas_call(
        functools.partial(_fox_prompt_kernel, tile=tile),
        grid=(pairs, m // tile),
        in_specs=[
            pl.BlockSpec((tile, LANES), lambda p, i: (i, p)),
            pl.BlockSpec((tile, LANES), lambda p, i: (i, 0)),
            pl.BlockSpec((m, LANES), lambda p, i: (0, p)),
            pl.BlockSpec((m, LANES), lambda p, i: (0, 0)),
            pl.BlockSpec((LANES, m), lambda p, i: (p, 0)),
        ],
        out_specs=pl.BlockSpec((tile, LANES), lambda p, i: (i, p)),
        out_shape=jax.ShapeDtypeStruct((m, FOX_DIM), BF16),
        compiler_params=_params("parallel", "arbitrary"),
        name="fox_prompt",
    )(q16, ce, k16, ce, vt16)


def _fox_sample_kernel(q_ref, kn_ref, vn_ref, cen_ref, kc_ref, vc_ref, cec_ref, o_ref, k_all, v_all):
    pair = pl.program_id(1)
    past = kc_ref.shape[1]
    seq = q_ref.shape[0]
    total = k_all.shape[0]
    k_all[0:past, 0:LANES] = kc_ref[0].astype(BF16)
    k_all[0:past, LANES:2 * LANES] = cec_ref[0]
    k_all[past:past + seq, 0:LANES] = kn_ref[...]
    k_all[past:past + seq, LANES:2 * LANES] = cen_ref[...]
    k_all[past + seq:total, :] = jnp.zeros((total - past - seq, 2 * LANES), BF16)
    v_all[0:past, :] = vc_ref[0].astype(BF16)
    v_all[past:past + seq, :] = vn_ref[...]
    v_all[past + seq:total, :] = jnp.zeros((total - past - seq, LANES), BF16)

    outs = []
    for half in range(2):
        q_op = _fox_query_operand(q_ref[...], cen_ref[...], 2 * pair + half, half)
        s = lax.dot_general(q_op, k_all[...], NT_DIMS, preferred_element_type=F32)
        qr = lax.broadcasted_iota(jnp.int32, s.shape, 0)
        kc = lax.broadcasted_iota(jnp.int32, s.shape, 1)
        s = jnp.where(kc <= qr + past, s, -jnp.inf)
        p = jnp.exp(s - jnp.max(s, axis=1, keepdims=True))
        l = jnp.sum(p, axis=1, keepdims=True)
        outs.append(jnp.dot(p.astype(BF16), v_all[...], preferred_element_type=F32) / l)
    lane = lax.broadcasted_iota(jnp.int32, outs[0].shape, 1)
    o_ref[...] = jnp.where(lane < FOX_HEAD_DIM, outs[0], outs[1]).astype(BF16)


def _fox_sample(q16, k16, v16, ce_new, cache_k, cache_v, ce_cache, batch, seq):
    past = cache_k.shape[1]
    total = -(-(past + seq) // LANES) * LANES
    pairs = FOX_HEADS // 2
    new = pl.BlockSpec((seq, LANES), lambda b, p: (b, p))
    cache = pl.BlockSpec((1, past, LANES), lambda b, p: (b, 0, p))
    return pl.pallas_call(
        _fox_sample_kernel,
        grid=(batch, pairs),
        in_specs=[new, new, new, pl.BlockSpec((seq, LANES), lambda b, p: (b, 0)),
                  cache, cache, pl.BlockSpec((1, past, LANES), lambda b, p: (b, 0, 0))],
        out_specs=new,
        out_shape=jax.ShapeDtypeStruct((batch * seq, FOX_DIM), BF16),
        scratch_shapes=[pltpu.VMEM((total, 2 * LANES), BF16), pltpu.VMEM((total, LANES), BF16)],
        compiler_params=_params("parallel", "arbitrary"),
        name="fox_sample",
    )(q16, k16, v16, ce_new, cache_k, cache_v, ce_cache)


def _merge_kernel(oa_ref, wa_ref, ob_ref, wb_ref, ga_ref, gb_ref, o_ref):
    a = jnp.dot(oa_ref[...], wa_ref[...], preferred_element_type=F32)
    b = jnp.dot(ob_ref[...], wb_ref[...], preferred_element_type=F32)
    ga = jax.nn.sigmoid(ga_ref[...].astype(F32))
    gb = jax.nn.sigmoid(gb_ref[...].astype(F32))
    o_ref[...] = (ga * a + gb * b).astype(BF16)


def _merge(o_gla, w_a, o_fox, w_b, gates):
    m = o_gla.shape[0]
    tm, tn = min(m, 1024), 512
    nb = D_MODEL // tn
    act = pl.BlockSpec((tm, D_MODEL), lambda i, j: (i, 0))
    wgt = pl.BlockSpec((D_MODEL, tn), lambda i, j: (0, j))
    return pl.pallas_call(
        _merge_kernel,
        grid=(m // tm, nb),
        in_specs=[act, wgt, act, wgt,
                  pl.BlockSpec((tm, tn), lambda i, j: (i, j)),
                  pl.BlockSpec((tm, tn), lambda i, j: (i, j + nb))],
        out_specs=pl.BlockSpec((tm, tn), lambda i, j: (i, j)),
        out_shape=jax.ShapeDtypeStruct((m, D_MODEL), BF16),
        compiler_params=_params("parallel", "arbitrary"),
        name="merge",
    )(o_gla, w_a, o_fox, w_b, gates, gates)


def _out_proj_kernel(a_ref, w_ref, x_ref, o_ref):
    o_ref[...] = x_ref[...] + jnp.dot(a_ref[...], w_ref[...], preferred_element_type=F32)


def _out_proj(merged, w, x):
    m = x.shape[0]
    tm, tn = min(m, 1024), 512
    return pl.pallas_call(
        _out_proj_kernel,
        grid=(m // tm, D_MODEL // tn),
        in_specs=[pl.BlockSpec((tm, D_MODEL), lambda i, j: (i, 0)),
                  pl.BlockSpec((D_MODEL, tn), lambda i, j: (0, j)),
                  pl.BlockSpec((tm, tn), lambda i, j: (i, j))],
        out_specs=pl.BlockSpec((tm, tn), lambda i, j: (i, j)),
        out_shape=jax.ShapeDtypeStruct((m, D_MODEL), F32),
        compiler_params=_params("parallel", "arbitrary"),
        name="out_proj",
    )(merged, w, x)


def _mlp_kernel(x_ref, g_ref, wu_ref, wd_ref, o_ref, h_ref):
    @pl.when(pl.program_id(1) == 0)
    def _():
        _rmsnorm_to_bf16(x_ref, g_ref, h_ref)
        o_ref[...] = x_ref[...]

    u = jnp.maximum(jnp.dot(h_ref[...], wu_ref[...], preferred_element_type=F32), 0.0)
    o_ref[...] += jnp.dot((u * u).astype(BF16), wd_ref[...], preferred_element_type=F32)


def _mlp(x, g, w_up, w_down):
    m = x.shape[0]
    tm, tf = min(m, 512), 512
    return pl.pallas_call(
        _mlp_kernel,
        grid=(m // tm, D_FF // tf),
        in_specs=[pl.BlockSpec((tm, D_MODEL), lambda i, f: (i, 0)),
                  pl.BlockSpec((1, D_MODEL), lambda i, f: (0, 0)),
                  pl.BlockSpec((D_MODEL, tf), lambda i, f: (0, f)),
                  pl.BlockSpec((tf, D_MODEL), lambda i, f: (f, 0))],
        out_specs=pl.BlockSpec((tm, D_MODEL), lambda i, f: (i, 0)),
        out_shape=jax.ShapeDtypeStruct((m, D_MODEL), F32),
        scratch_shapes=[pltpu.VMEM((tm, D_MODEL), BF16)],
        compiler_params=_params("parallel", "arbitrary"),
        name="mlp",
    )(x, g, w_up, w_down)


def _prepare_layer(l, norm_mix_g, w_in, gla_alpha_up, gla_alpha_b, gla_norm_g, fox_f_b, fox_qnorm_g,
                   fox_knorm_g, w_br_gla, w_br_fox, w_out, norm_mlp_g, w_up, w_down):
    w = w_in[l]
    o = 0
    cols = {}
    for name, size in (("gla", 2 * GLA_DK + 2 * GLA_DV), ("low", GLA_LOWRANK), ("fq", FOX_DIM),
                       ("fk", FOX_DIM), ("fv", FOX_DIM), ("ff", FOX_HEADS), ("gate", 2 * D_MODEL)):
        cols[name] = w[:, o:o + size]
        o += size
    pad = LANES - CE_ONES - GLA_LOWRANK
    w_small = jnp.concatenate([cols["ff"]] * CE_PARTS + [cols["low"], jnp.zeros((D_MODEL, pad), F32)], axis=1)
    b_small = jnp.concatenate([fox_f_b[l]] * CE_PARTS + [jnp.zeros((LANES - CE_ONES,), F32)])[None, :]
    alpha_pad = jnp.zeros((LANES, GLA_DK), F32).at[LOWRANK_LANE:LOWRANK_LANE + GLA_LOWRANK].set(gla_alpha_up[l])
    return dict(
        norm_mix_g=norm_mix_g[l][None, :], norm_mlp_g=norm_mlp_g[l][None, :],
        w_gla=cols["gla"].astype(BF16), w_small=w_small.astype(BF16), b_small=b_small,
        w_fq=cols["fq"].astype(BF16), w_fk=cols["fk"].astype(BF16), w_fv=cols["fv"].astype(BF16),
        w_gate=cols["gate"].astype(BF16),
        alpha_pad=alpha_pad.astype(BF16), alpha_b=gla_alpha_b[l][None, :], gla_norm_g=gla_norm_g[l][None, :],
        gq=jnp.tile(fox_qnorm_g[l], FOX_HEADS)[None, :], gk=jnp.tile(fox_knorm_g[l], FOX_HEADS)[None, :],
        w_br_gla=w_br_gla[l].astype(BF16), w_br_fox=w_br_fox[l].astype(BF16), w_out=w_out[l].astype(BF16),
        w_up=w_up[l].astype(BF16), w_down=w_down[l].astype(BF16),
    )


def _group_sum_matrix():
    r = lax.broadcasted_iota(jnp.int32, (MXU_DIM, MXU_DIM), 0) // FOX_HEAD_DIM
    c = lax.broadcasted_iota(jnp.int32, (MXU_DIM, MXU_DIM), 1) // FOX_HEAD_DIM
    return (r == c).astype(BF16)


def _layer(x, batch, seq, cache, p, gsum):
    g = p["norm_mix_g"]
    pg = _proj(x, g, p["w_gla"], "plain")
    small = _proj(x, g, p["w_small"], "small", (p["b_small"],))
    q16 = _proj(x, g, p["w_fq"], "qnorm", (gsum, p["gq"]))
    k32, k16 = _proj(x, g, p["w_fk"], "knorm", (gsum, p["gk"]))
    v32, v16 = _proj(x, g, p["w_fv"], "dual")
    gates = _proj(x, g, p["w_gate"], "plain")

    if cache is None:
        assert batch == 1
        o_gla, s_new = _gla(pg, small, p["alpha_pad"], p["alpha_b"], p["gla_norm_g"], None, batch, seq)
        ce = _ce_prompt(small)
        o_fox = _fox_prompt(q16, k16, v16.T, ce)
    else:
        cache_k, cache_v, cache_logf, s0 = cache
        past = cache_k.shape[1]
        o_gla, s_new = _gla(pg, small, p["alpha_pad"], p["alpha_b"], p["gla_norm_g"], s0, batch, seq)
        rep = jnp.concatenate([cache_logf] * CE_PARTS + [jnp.zeros((batch, past, LANES - CE_ONES), F32)], axis=-1)
        ce_cache, ce_new = _ce_sample(rep, small, batch, seq)
        o_fox = _fox_sample(q16, k16, v16, ce_new, cache_k.reshape(batch, past, FOX_DIM),
                            cache_v.reshape(batch, past, FOX_DIM), ce_cache, batch, seq)

    merged = _merge(o_gla, p["w_br_gla"], o_fox, p["w_br_fox"], gates)
    x = _out_proj(merged, p["w_out"], x)
    x = _mlp(x, p["norm_mlp_g"], p["w_up"], p["w_down"])
    k_out = k32.reshape(batch, seq, FOX_HEADS, FOX_HEAD_DIM)
    v_out = v32.reshape(batch, seq, FOX_HEADS, FOX_HEAD_DIM)
    logf = small[:, :FOX_HEADS].reshape(batch, seq, FOX_HEADS)
    return x, k_out, v_out, logf, s_new


def kernel(x_prompt, x_sample, cache_fox_k, cache_fox_v, cache_fox_logf, state_gla, norm_mix_g, w_in,
           gla_alpha_up, gla_alpha_b, gla_norm_g, fox_f_b, fox_qnorm_g, fox_knorm_g, w_br_gla, w_br_fox,
           w_out, norm_mlp_g, w_up, w_down):
    bp, lp, _ = x_prompt.shape
    bs, ls, _ = x_sample.shape
    xp = x_prompt.reshape(bp * lp, D_MODEL)
    xs = x_sample.reshape(bs * ls, D_MODEL)
    gsum = _group_sum_matrix()
    outs_p, outs_s = [], []
    for l in range(w_in.shape[0]):
        p = _prepare_layer(l, norm_mix_g, w_in, gla_alpha_up, gla_alpha_b, gla_norm_g, fox_f_b, fox_qnorm_g,
                           fox_knorm_g, w_br_gla, w_br_fox, w_out, norm_mlp_g, w_up, w_down)
        xp, *rest_p = _layer(xp, bp, lp, None, p, gsum)
        xs, *rest_s = _layer(xs, bs, ls, (cache_fox_k[l], cache_fox_v[l], cache_fox_logf[l], state_gla[l]), p, gsum)
        outs_p.append(rest_p)
        outs_s.append(rest_s)
    stack = lambda outs, i: jnp.stack([o[i] for o in outs])
    return (xp.reshape(bp, lp, D_MODEL), xs.reshape(bs, ls, D_MODEL),
            stack(outs_p, 0), stack(outs_p, 1), stack(outs_p, 2), stack(outs_p, 3),
            stack(outs_s, 0), stack(outs_s, 1), stack(outs_s, 2), stack(outs_s, 3))
```

```python
import functools

import jax
import jax.numpy as jnp
from jax import lax
from jax.experimental import pallas as pl
from jax.experimental.pallas import tpu as pltpu

F32 = jnp.float32
BF16 = jnp.bfloat16

D_MODEL = 2048
DEPTH = 4
GLA_HEADS = 4
GLA_HEAD_K = 256
GLA_HEAD_V = 512
GLA_DK = GLA_HEADS * GLA_HEAD_K
GLA_DV = GLA_HEADS * GLA_HEAD_V
GLA_LOWRANK = 16
GLA_GATE_NORMALIZER = 16.0
GLA_CHUNK = 64
GLA_SUB = 16
FOX_HEADS = 32
FOX_HEAD_DIM = 64
FOX_DIM = FOX_HEADS * FOX_HEAD_DIM
D_FF = 4 * D_MODEL
EPS = 1e-6

LANES = 128
MXU_DIM = 256
VMEM_LIMIT_BYTES = 56 * 1024 * 1024

CE_PARTS = 3
CE_ONES = CE_PARTS * FOX_HEADS
LOWRANK_LANE = CE_ONES

FOX_QUERY_TILE = MXU_DIM
FOX_QUERY_BLOCK = 1024
FOX_KEY_BLOCK = 512
FOX_SUM_ROWS = 16
LOG2E = 1.4426950408889634

NT_DIMS = (((1,), (1,)), ((), ()))
TN_DIMS = (((0,), (0,)), ((), ()))


def _params(*semantics):
    return pltpu.CompilerParams(dimension_semantics=semantics, vmem_limit_bytes=VMEM_LIMIT_BYTES)


def _log_sigmoid(x):
    return jnp.minimum(x, 0.0) - jnp.log(1.0 + jnp.exp(-jnp.abs(x)))


def _split3(x):
    p1 = x.astype(BF16)
    r1 = x - p1.astype(F32)
    p2 = r1.astype(BF16)
    p3 = (r1 - p2.astype(F32)).astype(BF16)
    return p1, p2, p3


def _lower_tri(n):
    r = lax.broadcasted_iota(jnp.int32, (n, n), 0)
    c = lax.broadcasted_iota(jnp.int32, (n, n), 1)
    return jnp.where(r >= c, 1.0, 0.0).astype(BF16)


def _cumsum_rows(tri, x):
    p1, p2, p3 = _split3(x)
    return (jnp.dot(tri, p1, preferred_element_type=F32)
            + jnp.dot(tri, p2, preferred_element_type=F32)
            + jnp.dot(tri, p3, preferred_element_type=F32))


def _rmsnorm_to_bf16(x_ref, g_ref, h_ref, rows=128):
    def body(r, carry):
        sl = pl.ds(pl.multiple_of(r * rows, rows), rows)
        x = x_ref[sl, :]
        ms = jnp.mean(x * x, axis=-1, keepdims=True)
        h_ref[sl, :] = (x * lax.rsqrt(ms + EPS) * g_ref[...]).astype(BF16)
        return carry
    lax.fori_loop(0, x_ref.shape[0] // rows, body, 0)


def _head_rms(acc, gsum_ref, gain):
    sq = acc * acc
    hi = sq.astype(BF16)
    lo = (sq - hi.astype(F32)).astype(BF16)
    parts = []
    for s in range(acc.shape[1] // MXU_DIM):
        sl = slice(s * MXU_DIM, (s + 1) * MXU_DIM)
        parts.append(jnp.dot(hi[:, sl], gsum_ref[...], preferred_element_type=F32)
                     + jnp.dot(lo[:, sl], gsum_ref[...], preferred_element_type=F32))
    ms = jnp.concatenate(parts, axis=1) * (1.0 / FOX_HEAD_DIM)
    return acc * lax.rsqrt(ms + EPS) * gain


def _proj_kernel(x_ref, g_ref, w_ref, *refs, mode):
    h_ref = refs[-1]

    @pl.when(pl.program_id(1) == 0)
    def _():
        _rmsnorm_to_bf16(x_ref, g_ref, h_ref)

    acc = jnp.dot(h_ref[...], w_ref[...], preferred_element_type=F32)
    if mode == "plain":
        (o_ref,) = refs[:-1]
        o_ref[...] = acc.astype(o_ref.dtype)
    elif mode == "small":
        b_ref, o_ref = refs[:-1]
        lane = lax.broadcasted_iota(jnp.int32, acc.shape, 1)
        o_ref[...] = jnp.where(lane < CE_ONES, _log_sigmoid(acc + b_ref[...]), acc)
    elif mode == "qnorm":
        gsum_ref, gain_ref, o_ref = refs[:-1]
        o_ref[...] = (_head_rms(acc, gsum_ref, gain_ref[...]) * (FOX_HEAD_DIM ** -0.5 * LOG2E)).astype(BF16)
    elif mode == "knorm":
        gsum_ref, gain_ref, o32_ref, o16_ref = refs[:-1]
        y = _head_rms(acc, gsum_ref, gain_ref[...])
        o32_ref[...] = y
        o16_ref[...] = y.astype(BF16)
    elif mode == "dual":
        o32_ref, o16_ref = refs[:-1]
        o32_ref[...] = acc
        o16_ref[...] = acc.astype(BF16)
    else:
        o32_ref, o16t_ref = refs[:-1]
        o32_ref[...] = acc
        o16t_ref[...] = acc.T.astype(BF16)


def _proj(x, g, w, mode, extras=()):
    m, n = x.shape[0], w.shape[1]
    tm = min(m, 1024)
    tn = min(n, 512)
    grid = (m // tm, n // tn)
    in_specs = [
        pl.BlockSpec((tm, D_MODEL), lambda i, j: (i, 0)),
        pl.BlockSpec((1, D_MODEL), lambda i, j: (0, 0)),
        pl.BlockSpec((D_MODEL, tn), lambda i, j: (0, j)),
    ]
    tile = pl.BlockSpec((tm, tn), lambda i, j: (i, j))
    row = pl.BlockSpec((1, tn), lambda i, j: (0, j))
    if mode == "plain":
        out_shape, out_specs = jax.ShapeDtypeStruct((m, n), BF16), tile
    elif mode == "small":
        in_specs += [row]
        out_shape, out_specs = jax.ShapeDtypeStruct((m, n), F32), tile
    elif mode == "qnorm":
        in_specs += [pl.BlockSpec((MXU_DIM, MXU_DIM), lambda i, j: (0, 0)), row]
        out_shape, out_specs = jax.ShapeDtypeStruct((m, n), BF16), tile
    elif mode == "dual_t":
        out_shape = (jax.ShapeDtypeStruct((m, n), F32), jax.ShapeDtypeStruct((n, m), BF16))
        out_specs = (tile, pl.BlockSpec((tn, tm), lambda i, j: (j, i)))
    else:
        if mode == "knorm":
            in_specs += [pl.BlockSpec((MXU_DIM, MXU_DIM), lambda i, j: (0, 0)), row]
        out_shape = (jax.ShapeDtypeStruct((m, n), F32), jax.ShapeDtypeStruct((m, n), BF16))
        out_specs = (tile, tile)
    return pl.pallas_call(
        functools.partial(_proj_kernel, mode=mode),
        grid=grid, in_specs=in_specs, out_specs=out_specs, out_shape=out_shape,
        scratch_shapes=[pltpu.VMEM((tm, D_MODEL), BF16)],
        compiler_params=_params("parallel", "arbitrary"),
        name="proj_" + mode,
    )(x, g, w, *extras)


def _encode_cumsum(cs):
    c1, c2, c3 = _split3(cs * LOG2E)
    lane = lax.broadcasted_iota(jnp.int32, cs.shape, 1)
    one = jnp.ones_like(c1)
    return jnp.where(lane < FOX_HEADS, c1,
                     jnp.where(lane < 2 * FOX_HEADS, c2, jnp.where(lane < CE_ONES, c3, one)))


def _ce_prompt_kernel(s_ref, ce_ref, carry_ref):
    @pl.when(pl.program_id(0) == 0)
    def _():
        carry_ref[...] = jnp.zeros_like(carry_ref)

    t = s_ref.shape[0]
    cs = _cumsum_rows(_lower_tri(t), s_ref[...]) + carry_ref[...]
    carry_ref[...] = cs[t - 1:t, :]
    ce_ref[...] = _encode_cumsum(cs)


def _ce_prompt(small):
    m = small.shape[0]
    t = MXU_DIM
    return pl.pallas_call(
        _ce_prompt_kernel,
        grid=(m // t,),
        in_specs=[pl.BlockSpec((t, LANES), lambda i: (i, 0))],
        out_specs=pl.BlockSpec((t, LANES), lambda i: (i, 0)),
        out_shape=jax.ShapeDtypeStruct((m, LANES), BF16),
        scratch_shapes=[pltpu.VMEM((1, LANES), F32)],
        compiler_params=_params("arbitrary"),
        name="ce_prompt",
    )(small)


def _ce_sample_kernel(c_ref, s_ref, cec_ref, cen_ref):
    t = MXU_DIM
    tri = _lower_tri(t)
    carry = jnp.zeros((1, LANES), F32)
    for b in range(c_ref.shape[1] // t):
        cs = _cumsum_rows(tri, c_ref[0, b * t:(b + 1) * t, :]) + carry
        carry = cs[t - 1:t, :]
        cec_ref[0, b * t:(b + 1) * t, :] = _encode_cumsum(cs)
    n = s_ref.shape[0]
    cs = _cumsum_rows(_lower_tri(n), s_ref[...]) + carry
    cen_ref[...] = _encode_cumsum(cs)


def _ce_sample(cache_rep, small, batch, seq):
    past = cache_rep.shape[1]
    return pl.pallas_call(
        _ce_sample_kernel,
        grid=(batch,),
        in_specs=[pl.BlockSpec((1, past, LANES), lambda b: (b, 0, 0)),
                  pl.BlockSpec((seq, LANES), lambda b: (b, 0))],
        out_specs=(pl.BlockSpec((1, past, LANES), lambda b: (b, 0, 0)),
                   pl.BlockSpec((seq, LANES), lambda b: (b, 0))),
        out_shape=(jax.ShapeDtypeStruct((batch, past, LANES), BF16),
                   jax.ShapeDtypeStruct((batch * seq, LANES), BF16)),
        compiler_params=_params("parallel"),
        name="ce_sample",
    )(cache_rep, small)


def _gla_kernel(*refs, chunk, has_state):
    if has_state:
        (q_ref, k_ref, v_ref, gg_ref, sm_ref, au_ref, ab_ref, gn_ref, s0_ref,
         o_ref, sout_ref, st_ref) = refs
    else:
        (q_ref, k_ref, v_ref, gg_ref, sm_ref, au_ref, ab_ref, gn_ref,
         o_ref, sout_ref, st_ref) = refs
    blk = pl.program_id(1)
    hk, hv = GLA_HEAD_K, GLA_HEAD_V
    tb = q_ref.shape[0]
    n_chunks, n_sub = tb // chunk, chunk // GLA_SUB
    heads = range(GLA_HEADS)

    @pl.when(blk == 0)
    def _():
        for h in heads:
            st_ref[h] = s0_ref[0, h].T if has_state else jnp.zeros((hv, hk), F32)

    a = jnp.dot(sm_ref[...].astype(BF16), au_ref[...], preferred_element_type=F32) + ab_ref[...]
    log_a = _log_sigmoid(a) * (1.0 / GLA_GATE_NORMALIZER)
    r = lax.broadcasted_iota(jnp.int32, (tb, tb), 0)
    c = lax.broadcasted_iota(jnp.int32, (tb, tb), 1)
    tri = jnp.where((r >= c) & (r // chunk == c // chunk), 1.0, 0.0).astype(BF16)
    bc = _cumsum_rows(tri, log_a)

    def rows_like(row_of, group):
        return jnp.concatenate([jnp.broadcast_to(bc[row_of(g):row_of(g) + 1, :], (group, bc.shape[1]))
                                for g in range(tb // group)], axis=0)

    last = rows_like(lambda g: (g + 1) * chunk - 1, chunk)
    anchor = rows_like(lambda g: g * GLA_SUB, GLA_SUB)
    scale = hk ** -0.5
    q = q_ref[...].astype(F32)
    k = k_ref[...].astype(F32)
    q_inter = (q * jnp.exp(bc) * scale).astype(BF16)
    k_dec = (k * jnp.exp(last - bc)).astype(BF16)
    q_sub = (q * jnp.exp(bc - anchor) * scale).astype(BF16)

    att = {}
    for ch in range(n_chunks):
        r0 = ch * chunk
        q_slots, k_slots = [], []
        for i in range(n_sub):
            lo, hi = i * GLA_SUB, (i + 1) * GLA_SUB
            k_i = (k[r0:r0 + hi] * jnp.exp(anchor[r0 + lo:r0 + lo + 1] - bc[r0:r0 + hi])).astype(BF16)
            k_slots.append(jnp.concatenate([k_i, jnp.zeros((chunk - hi, k.shape[1]), BF16)], axis=0)
                           if hi < chunk else k_i)
            pieces = [jnp.zeros((lo, q.shape[1]), BF16)] if lo else []
            pieces.append(q_sub[r0 + lo:r0 + hi])
            if hi < chunk:
                pieces.append(jnp.zeros((chunk - hi, q.shape[1]), BF16))
            q_slots.append(jnp.concatenate(pieces, axis=0) if len(pieces) > 1 else pieces[0])
        for h in heads:
            kc = slice(h * hk, (h + 1) * hk)
            q_hat = jnp.concatenate([s[:, kc] for s in q_slots], axis=1)
            k_hat = jnp.concatenate([s[:, kc] for s in k_slots], axis=1)
            att[ch, h] = lax.dot_general(q_hat, k_hat, NT_DIMS, preferred_element_type=F32)
    ri = lax.broadcasted_iota(jnp.int32, (chunk, chunk), 0)
    ci = lax.broadcasted_iota(jnp.int32, (chunk, chunk), 1)
    o_intra = {}
    for (ch, h), s in att.items():
        p = jnp.where(ci <= ri, s, 0.0).astype(BF16)
        o_intra[ch, h] = jnp.dot(p, v_ref[ch * chunk:(ch + 1) * chunk, h * hv:(h + 1) * hv],
                                 preferred_element_type=F32)

    for ch in range(n_chunks):
        rows = slice(ch * chunk, (ch + 1) * chunk)
        decay = jnp.exp(bc[(ch + 1) * chunk - 1:(ch + 1) * chunk, :])
        for h in heads:
            kc = slice(h * hk, (h + 1) * hk)
            vc = slice(h * hv, (h + 1) * hv)
            st = st_ref[h]
            o = o_intra[ch, h] + lax.dot_general(q_inter[rows, kc], st.astype(BF16), NT_DIMS,
                                                 preferred_element_type=F32)
            st_ref[h] = st * decay[:, kc] + lax.dot_general(v_ref[rows, vc], k_dec[rows, kc], TN_DIMS,
                                                            preferred_element_type=F32)
            ms = jnp.mean(o * o, axis=-1, keepdims=True)
            y = o * lax.rsqrt(ms + EPS) * gn_ref[...]
            gg = gg_ref[rows, vc].astype(F32)
            o_ref[rows, vc] = (y * (gg * jax.nn.sigmoid(gg))).astype(BF16)

    @pl.when(blk == pl.num_programs(1) - 1)
    def _():
        for h in heads:
            sout_ref[0, h] = st_ref[h].T


def _gla(pg, small, alpha_pad, alpha_b, gnorm, s0, batch, seq):
    chunk = min(GLA_CHUNK, seq)
    tb = min(seq, 4 * chunk)
    nblk = seq // tb
    hk, hv = GLA_HEAD_K, GLA_HEAD_V
    row = lambda b, t: b * nblk + t
    in_specs = [
        pl.BlockSpec((tb, GLA_DK), lambda b, t: (row(b, t), 0)),
        pl.BlockSpec((tb, GLA_DK), lambda b, t: (row(b, t), 1)),
        pl.BlockSpec((tb, GLA_DV), lambda b, t: (row(b, t), 1)),
        pl.BlockSpec((tb, GLA_DV), lambda b, t: (row(b, t), 2)),
        pl.BlockSpec((tb, LANES), lambda b, t: (row(b, t), 0)),
        pl.BlockSpec((LANES, GLA_DK), lambda b, t: (0, 0)),
        pl.BlockSpec((1, GLA_DK), lambda b, t: (0, 0)),
        pl.BlockSpec((1, hv), lambda b, t: (0, 0)),
    ]
    args = [pg, pg, pg, pg, small, alpha_pad, alpha_b, gnorm]
    state_spec = pl.BlockSpec((1, GLA_HEADS, hk, hv), lambda b, t: (b, 0, 0, 0))
    if s0 is not None:
        in_specs.append(state_spec)
        args.append(s0)
    return pl.pallas_call(
        functools.partial(_gla_kernel, chunk=chunk, has_state=s0 is not None),
        grid=(batch, nblk),
        in_specs=in_specs,
        out_specs=(pl.BlockSpec((tb, GLA_DV), lambda b, t: (row(b, t), 0)), state_spec),
        out_shape=(jax.ShapeDtypeStruct((batch * seq, GLA_DV), BF16),
                   jax.ShapeDtypeStruct((batch, GLA_HEADS, hk, hv), F32)),
        scratch_shapes=[pltpu.VMEM((GLA_HEADS, hv, hk), F32)],
        compiler_params=_params("parallel", "arbitrary"),
        name="gla",
    )(*args)


def _fox_query_operand(q, ce_q, head, half):
    lane = lax.broadcasted_iota(jnp.int32, (1, LANES), 1)
    keep = jnp.where((lane >= half * FOX_HEAD_DIM) & (lane < (half + 1) * FOX_HEAD_DIM), 1.0, 0.0)
    q_m = q * keep.astype(BF16)
    r = lax.broadcasted_iota(jnp.int32, (LANES, LANES), 0)
    c = lax.broadcasted_iota(jnp.int32, (LANES, LANES), 1)
    pick = jnp.zeros((LANES, LANES), F32)
    for part in range(CE_PARTS):
        pick = jnp.where((r == head + part * FOX_HEADS) & (c == CE_ONES + part), 1.0, pick)
    enc = jnp.dot(ce_q, pick.astype(BF16), preferred_element_type=F32)
    minus = jnp.zeros((1, LANES), F32)
    for part in range(CE_PARTS):
        minus = jnp.where(lane == head + part * FOX_HEADS, -1.0, minus)
    return jnp.concatenate([q_m, (enc + minus).astype(BF16)], axis=1)


def _fox_prompt_kernel(q_ref, ceq_ref, k_ref, cek_ref, vt_ref, o_ref, st_ref, *, tq, tk):
    pair = pl.program_id(0)
    qi = pl.program_id(1)
    qt = FOX_QUERY_TILE
    nt = tq // qt
    tail_blocks = tq // tk
    chains = [(half, r) for half in range(2) for r in range(nt)]
    everyone = list(range(len(chains)))
    q_ops = [_fox_query_operand(q_ref[r * qt:(r + 1) * qt, :], ceq_ref[r * qt:(r + 1) * qt, :],
                                2 * pair + half, half) for half, r in chains]

    def step(block, slot, carries, attend, score_next, tail_block=None):
        carries = list(carries)
        if attend:
            ks = pl.multiple_of(block * tk, tk)
            ones = jnp.ones((FOX_SUM_ROWS, tk), BF16)
            vts = [jnp.concatenate([vt_ref[h * FOX_HEAD_DIM:(h + 1) * FOX_HEAD_DIM, pl.ds(ks, tk)], ones],
                                   axis=0) for h in range(2)]
        if score_next:
            kn = pl.multiple_of((block + 1) * tk, tk)
            k_op = jnp.concatenate([k_ref[pl.ds(kn, tk), :], cek_ref[pl.ds(kn, tk), :]], axis=1)
        for n in everyone:
            if n in score_next:
                st_ref[1 - slot, n] = lax.dot_general(k_op, q_ops[n], NT_DIMS, preferred_element_type=F32)
            if n in attend:
                half, r = chains[n]
                m, acc = carries[n]
                st = st_ref[slot, n]
                if tail_block is not None:
                    key = lax.broadcasted_iota(jnp.int32, st.shape, 0) + (tail_block * tk - r * qt)
                    qry = lax.broadcasted_iota(jnp.int32, st.shape, 1)
                    st = jnp.where(key <= qry, st, -jnp.inf)
                m_new = jnp.maximum(m, jnp.max(st, axis=0, keepdims=True))
                p = jnp.exp2(st - m_new).astype(BF16)
                acc = jnp.exp2(m - m_new) * acc + jnp.dot(vts[half], p, preferred_element_type=F32)
                carries[n] = (m_new, acc)
        return tuple(carries)

    def body(jj, carries):
        carries = step(2 * jj, 0, carries, everyone, everyone)
        return step(2 * jj + 1, 1, carries, everyone, everyone)

    init = tuple((jnp.full((1, qt), -jnp.inf, F32), jnp.zeros((FOX_HEAD_DIM + FOX_SUM_ROWS, qt), F32))
                 for _ in chains)
    full_blocks = qi * tail_blocks
    carries = step(-1, 1, init, [], everyone)
    carries = lax.fori_loop(0, full_blocks // 2, body, carries)

    needs = lambda t: [n for n in everyone if t < tail_blocks and (chains[n][1] + 1) * qt > t * tk]
    for t in range(tail_blocks):
        carries = step(full_blocks + t, t % 2, carries, needs(t), needs(t + 1), tail_block=t)

    for r in range(nt):
        outs = []
        for half in range(2):
            _, acc = carries[chains.index((half, r))]
            outs.append(acc[:FOX_HEAD_DIM] / acc[FOX_HEAD_DIM:FOX_HEAD_DIM + 1])
        o_ref[r * qt:(r + 1) * qt, :] = jnp.concatenate(outs, axis=0).T.astype(BF16)


def _fox_prompt(q16, k16, vt16, ce):
    m = q16.shape[0]
    tq = min(m, FOX_QUERY_BLOCK)
    tk = min(tq // 2, FOX_KEY_BLOCK)
    assert (tq // tk) % 2 == 0 and tk % FOX_QUERY_TILE == 0
    pairs = FOX_HEADS // 2
    chains = 2 * (tq // FOX_QUERY_TILE)
    return pl.pallas_call(
        functools.partial(_fox_prompt_kernel, tq=tq, tk=tk),
        grid=(pairs, m // tq),
        in_specs=[
            pl.BlockSpec((tq, LANES), lambda p, i: (i, p)),
            pl.BlockSpec((tq, LANES), lambda p, i: (i, 0)),
            pl.BlockSpec((m, LANES), lambda p, i: (0, p)),
            pl.BlockSpec((m, LANES), lambda p, i: (0, 0)),
            pl.BlockSpec((LANES, m), lambda p, i: (p, 0)),
        ],
        out_specs=pl.BlockSpec((tq, LANES), lambda p, i: (i, p)),
        out_shape=jax.ShapeDtypeStruct((m, FOX_DIM), BF16),
        scratch_shapes=[pltpu.VMEM((2, chains, tk, FOX_QUERY_TILE), F32)],
        compiler_params=_params("parallel", "arbitrary"),
        name="fox_prompt",
    )(q16, ce, k16, ce, vt16)


def _fox_sample_kernel(q_ref, kn_ref, vn_ref, cen_ref, kc_ref, vc_ref, cec_ref, o_ref, k_all, v_all):
    pair = pl.program_id(1)
    past = kc_ref.shape[1]
    seq = q_ref.shape[0]
    total = k_all.shape[0]
    k_all[0:past, 0:LANES] = kc_ref[0].astype(BF16)
    k_all[0:past, LANES:2 * LANES] = cec_ref[0]
    k_all[past:past + seq, 0:LANES] = kn_ref[...]
    k_all[past:past + seq, LANES:2 * LANES] = cen_ref[...]
    k_all[past + seq:total, :] = jnp.zeros((total - past - seq, 2 * LANES), BF16)
    v_all[0:past, :] = vc_ref[0].astype(BF16)
    v_all[past:past + seq, :] = vn_ref[...]
    v_all[past + seq:total, :] = jnp.zeros((total - past - seq, LANES), BF16)

    q_op = jnp.concatenate([_fox_query_operand(q_ref[...], cen_ref[...], 2 * pair + half, half)
                            for half in range(2)], axis=0)
    s = lax.dot_general(q_op, k_all[...], NT_DIMS, preferred_element_type=F32)
    qr = lax.broadcasted_iota(jnp.int32, s.shape, 0)
    qr = jnp.where(qr >= seq, qr - seq, qr)
    kc = lax.broadcasted_iota(jnp.int32, s.shape, 1)
    s = jnp.where(kc <= qr + past, s, -jnp.inf)
    p = jnp.exp2(s - jnp.max(s, axis=1, keepdims=True))
    l = jnp.sum(p, axis=1, keepdims=True)
    o = jnp.dot(p.astype(BF16), v_all[...], preferred_element_type=F32) / l
    lane = lax.broadcasted_iota(jnp.int32, (seq, LANES), 1)
    o_ref[...] = jnp.where(lane < FOX_HEAD_DIM, o[:seq], o[seq:]).astype(BF16)


def _fox_sample(q16, k16, v16, ce_new, cache_k, cache_v, ce_cache, batch, seq):
    past = cache_k.shape[1]
    total = -(-(past + seq) // LANES) * LANES
    pairs = FOX_HEADS // 2
    new = pl.BlockSpec((seq, LANES), lambda b, p: (b, p))
    cache = pl.BlockSpec((1, past, LANES), lambda b, p: (b, 0, p))
    return pl.pallas_call(
        _fox_sample_kernel,
        grid=(batch, pairs),
        in_specs=[new, new, new, pl.BlockSpec((seq, LANES), lambda b, p: (b, 0)),
                  cache, cache, pl.BlockSpec((1, past, LANES), lambda b, p: (b, 0, 0))],
        out_specs=new,
        out_shape=jax.ShapeDtypeStruct((batch * seq, FOX_DIM), BF16),
        scratch_shapes=[pltpu.VMEM((total, 2 * LANES), BF16), pltpu.VMEM((total, LANES), BF16)],
        compiler_params=_params("parallel", "arbitrary"),
        name="fox_sample",
    )(q16, k16, v16, ce_new, cache_k, cache_v, ce_cache)


def _merge_kernel(oa_ref, wa_ref, ob_ref, wb_ref, ga_ref, gb_ref, o_ref):
    a = jnp.dot(oa_ref[...], wa_ref[...], preferred_element_type=F32)
    b = jnp.dot(ob_ref[...], wb_ref[...], preferred_element_type=F32)
    ga = jax.nn.sigmoid(ga_ref[...].astype(F32))
    gb = jax.nn.sigmoid(gb_ref[...].astype(F32))
    o_ref[...] = (ga * a + gb * b).astype(BF16)


def _merge(o_gla, w_a, o_fox, w_b, gates):
    m = o_gla.shape[0]
    tm, tn = min(m, 1024), 512
    nb = D_MODEL // tn
    act = pl.BlockSpec((tm, D_MODEL), lambda i, j: (i, 0))
    wgt = pl.BlockSpec((D_MODEL, tn), lambda i, j: (0, j))
    return pl.pallas_call(
        _merge_kernel,
        grid=(m // tm, nb),
        in_specs=[act, wgt, act, wgt,
                  pl.BlockSpec((tm, tn), lambda i, j: (i, j)),
                  pl.BlockSpec((tm, tn), lambda i, j: (i, j + nb))],
        out_specs=pl.BlockSpec((tm, tn), lambda i, j: (i, j)),
        out_shape=jax.ShapeDtypeStruct((m, D_MODEL), BF16),
        compiler_params=_params("parallel", "arbitrary"),
        name="merge",
    )(o_gla, w_a, o_fox, w_b, gates, gates)


def _out_proj_kernel(a_ref, w_ref, x_ref, o_ref):
    o_ref[...] = x_ref[...] + jnp.dot(a_ref[...], w_ref[...], preferred_element_type=F32)


def _out_proj(merged, w, x):
    m = x.shape[0]
    tm, tn = min(m, 1024), 512
    return pl.pallas_call(
        _out_proj_kernel,
        grid=(m // tm, D_MODEL // tn),
        in_specs=[pl.BlockSpec((tm, D_MODEL), lambda i, j: (i, 0)),
                  pl.BlockSpec((D_MODEL, tn), lambda i, j: (0, j)),
                  pl.BlockSpec((tm, tn), lambda i, j: (i, j))],
        out_specs=pl.BlockSpec((tm, tn), lambda i, j: (i, j)),
        out_shape=jax.ShapeDtypeStruct((m, D_MODEL), F32),
        compiler_params=_params("parallel", "arbitrary"),
        name="out_proj",
    )(merged, w, x)


def _mlp_kernel(x_ref, g_ref, wu_ref, wd_ref, o_ref, h_ref):
    @pl.when(pl.program_id(1) == 0)
    def _():
        _rmsnorm_to_bf16(x_ref, g_ref, h_ref)
        o_ref[...] = x_ref[...]

    u = jnp.maximum(jnp.dot(h_ref[...], wu_ref[...], preferred_element_type=F32), 0.0)
    o_ref[...] += jnp.dot((u * u).astype(BF16), wd_ref[...], preferred_element_type=F32)


def _mlp(x, g, w_up, w_down):
    m = x.shape[0]
    tm, tf = min(m, 512), 512
    return pl.pallas_call(
        _mlp_kernel,
        grid=(m // tm, D_FF // tf),
        in_specs=[pl.BlockSpec((tm, D_MODEL), lambda i, f: (i, 0)),
                  pl.BlockSpec((1, D_MODEL), lambda i, f: (0, 0)),
                  pl.BlockSpec((D_MODEL, tf), lambda i, f: (0, f)),
                  pl.BlockSpec((tf, D_MODEL), lambda i, f: (f, 0))],
        out_specs=pl.BlockSpec((tm, D_MODEL), lambda i, f: (i, 0)),
        out_shape=jax.ShapeDtypeStruct((m, D_MODEL), F32),
        scratch_shapes=[pltpu.VMEM((tm, D_MODEL), BF16)],
        compiler_params=_params("parallel", "arbitrary"),
        name="mlp",
    )(x, g, w_up, w_down)


def _prepare_layer(l, norm_mix_g, w_in, gla_alpha_up, gla_alpha_b, gla_norm_g, fox_f_b, fox_qnorm_g,
                   fox_knorm_g, w_br_gla, w_br_fox, w_out, norm_mlp_g, w_up, w_down):
    w = w_in[l]
    o = 0
    cols = {}
    for name, size in (("gla", 2 * GLA_DK + 2 * GLA_DV), ("low", GLA_LOWRANK), ("fq", FOX_DIM),
                       ("fk", FOX_DIM), ("fv", FOX_DIM), ("ff", FOX_HEADS), ("gate", 2 * D_MODEL)):
        cols[name] = w[:, o:o + size]
        o += size
    pad = LANES - CE_ONES - GLA_LOWRANK
    w_small = jnp.concatenate([cols["ff"]] * CE_PARTS + [cols["low"], jnp.zeros((D_MODEL, pad), F32)], axis=1)
    b_small = jnp.concatenate([fox_f_b[l]] * CE_PARTS + [jnp.zeros((LANES - CE_ONES,), F32)])[None, :]
    alpha_pad = jnp.zeros((LANES, GLA_DK), F32).at[LOWRANK_LANE:LOWRANK_LANE + GLA_LOWRANK].set(gla_alpha_up[l])
    return dict(
        norm_mix_g=norm_mix_g[l][None, :], norm_mlp_g=norm_mlp_g[l][None, :],
        w_gla=cols["gla"].astype(BF16), w_small=w_small.astype(BF16), b_small=b_small,
        w_fq=cols["fq"].astype(BF16), w_fk=cols["fk"].astype(BF16), w_fv=cols["fv"].astype(BF16),
        w_gate=cols["gate"].astype(BF16),
        alpha_pad=alpha_pad.astype(BF16), alpha_b=gla_alpha_b[l][None, :], gla_norm_g=gla_norm_g[l][None, :],
        gq=jnp.tile(fox_qnorm_g[l], FOX_HEADS)[None, :], gk=jnp.tile(fox_knorm_g[l], FOX_HEADS)[None, :],
        w_br_gla=w_br_gla[l].astype(BF16), w_br_fox=w_br_fox[l].astype(BF16), w_out=w_out[l].astype(BF16),
        w_up=w_up[l].astype(BF16), w_down=w_down[l].astype(BF16),
    )


def _group_sum_matrix():
    r = lax.broadcasted_iota(jnp.int32, (MXU_DIM, MXU_DIM), 0) // FOX_HEAD_DIM
    c = lax.broadcasted_iota(jnp.int32, (MXU_DIM, MXU_DIM), 1) // FOX_HEAD_DIM
    return (r == c).astype(BF16)


def _layer(x, batch, seq, cache, p, gsum):
    g = p["norm_mix_g"]
    pg = _proj(x, g, p["w_gla"], "plain")
    small = _proj(x, g, p["w_small"], "small", (p["b_small"],))
    q16 = _proj(x, g, p["w_fq"], "qnorm", (gsum, p["gq"]))
    k32, k16 = _proj(x, g, p["w_fk"], "knorm", (gsum, p["gk"]))
    v32, v16 = _proj(x, g, p["w_fv"], "dual_t" if cache is None else "dual")
    gates = _proj(x, g, p["w_gate"], "plain")

    if cache is None:
        assert batch == 1
        o_gla, s_new = _gla(pg, small, p["alpha_pad"], p["alpha_b"], p["gla_norm_g"], None, batch, seq)
        ce = _ce_prompt(small)
        o_fox = _fox_prompt(q16, k16, v16, ce)
    else:
        cache_k, cache_v, cache_logf, s0 = cache
        past = cache_k.shape[1]
        o_gla, s_new = _gla(pg, small, p["alpha_pad"], p["alpha_b"], p["gla_norm_g"], s0, batch, seq)
        rep = jnp.concatenate([cache_logf] * CE_PARTS + [jnp.zeros((batch, past, LANES - CE_ONES), F32)], axis=-1)
        ce_cache, ce_new = _ce_sample(rep, small, batch, seq)
        o_fox = _fox_sample(q16, k16, v16, ce_new, cache_k.reshape(batch, past, FOX_DIM),
                            cache_v.reshape(batch, past, FOX_DIM), ce_cache, batch, seq)

    merged = _merge(o_gla, p["w_br_gla"], o_fox, p["w_br_fox"], gates)
    x = _out_proj(merged, p["w_out"], x)
    x = _mlp(x, p["norm_mlp_g"], p["w_up"], p["w_down"])
    k_out = k32.reshape(batch, seq, FOX_HEADS, FOX_HEAD_DIM)
    v_out = v32.reshape(batch, seq, FOX_HEADS, FOX_HEAD_DIM)
    logf = small[:, :FOX_HEADS].reshape(batch, seq, FOX_HEADS)
    return x, k_out, v_out, logf, s_new


def kernel(x_prompt, x_sample, cache_fox_k, cache_fox_v, cache_fox_logf, state_gla, norm_mix_g, w_in,
           gla_alpha_up, gla_alpha_b, gla_norm_g, fox_f_b, fox_qnorm_g, fox_knorm_g, w_br_gla, w_br_fox,
           w_out, norm_mlp_g, w_up, w_down):
    bp, lp, _ = x_prompt.shape
    bs, ls, _ = x_sample.shape
    xp = x_prompt.reshape(bp * lp, D_MODEL)
    xs = x_sample.reshape(bs * ls, D_MODEL)
    gsum = _group_sum_matrix()
    outs_p, outs_s = [], []
    for l in range(w_in.shape[0]):
        p = _prepare_layer(l, norm_mix_g, w_in, gla_alpha_up, gla_alpha_b, gla_norm_g, fox_f_b, fox_qnorm_g,
                           fox_knorm_g, w_br_gla, w_br_fox, w_out, norm_mlp_g, w_up, w_down)
        xp, *rest_p = _layer(xp, bp, lp, None, p, gsum)
        xs, *rest_s = _layer(xs, bs, ls, (cache_fox_k[l], cache_fox_v[l], cache_fox_logf[l], state_gla[l]), p, gsum)
        outs_p.append(rest_p)
        outs_s.append(rest_s)
    stack = lambda outs, i: jnp.stack([o[i] for o in outs])
    return (xp.reshape(bp, lp, D_MODEL), xs.reshape(bs, ls, D_MODEL),
            stack(outs_p, 0), stack(outs_p, 1), stack(outs_p, 2), stack(outs_p, 3),
            stack(outs_s, 0), stack(outs_s, 1), stack(outs_s, 2), stack(outs_s, 3))
```

```python
import functools

import jax
import jax.numpy as jnp
from jax import lax
from jax.experimental import pallas as pl
from jax.experimental.pallas import tpu as pltpu

F32 = jnp.float32
BF16 = jnp.bfloat16

D_MODEL = 2048
DEPTH = 4
GLA_HEADS = 4
GLA_HEAD_K = 256
GLA_HEAD_V = 512
GLA_DK = GLA_HEADS * GLA_HEAD_K
GLA_DV = GLA_HEADS * GLA_HEAD_V
GLA_LOWRANK = 16
GLA_GATE_NORMALIZER = 16.0
GLA_CHUNK = 64
GLA_SUB = 16
FOX_HEADS = 32
FOX_HEAD_DIM = 64
FOX_DIM = FOX_HEADS * FOX_HEAD_DIM
D_FF = 4 * D_MODEL
EPS = 1e-6

LANES = 128
MXU_DIM = 256
VMEM_LIMIT_BYTES = 56 * 1024 * 1024

CE_PARTS = 3
CE_ONES = CE_PARTS * FOX_HEADS
LOWRANK_LANE = CE_ONES

FOX_QUERY_TILE = MXU_DIM
FOX_QUERY_BLOCK = 1024
FOX_KEY_BLOCK = 512
FOX_SUM_ROWS = 16
LOG2E = 1.4426950408889634
FOX_UNDERFLOW_LOG2 = -160.0
FOX_LOOP_UNROLL = 2

NT_DIMS = (((1,), (1,)), ((), ()))
TN_DIMS = (((0,), (0,)), ((), ()))


def _params(*semantics):
    return pltpu.CompilerParams(dimension_semantics=semantics, vmem_limit_bytes=VMEM_LIMIT_BYTES)


def _log_sigmoid(x):
    return jnp.minimum(x, 0.0) - jnp.log(1.0 + jnp.exp(-jnp.abs(x)))


def _split3(x):
    p1 = x.astype(BF16)
    r1 = x - p1.astype(F32)
    p2 = r1.astype(BF16)
    p3 = (r1 - p2.astype(F32)).astype(BF16)
    return p1, p2, p3


def _lower_tri(n):
    r = lax.broadcasted_iota(jnp.int32, (n, n), 0)
    c = lax.broadcasted_iota(jnp.int32, (n, n), 1)
    return jnp.where(r >= c, 1.0, 0.0).astype(BF16)


def _cumsum_rows(tri, x):
    p1, p2, p3 = _split3(x)
    return (jnp.dot(tri, p1, preferred_element_type=F32)
            + jnp.dot(tri, p2, preferred_element_type=F32)
            + jnp.dot(tri, p3, preferred_element_type=F32))


def _rmsnorm_to_bf16(x_ref, g_ref, h_ref, rows=128):
    def body(r, carry):
        sl = pl.ds(pl.multiple_of(r * rows, rows), rows)
        x = x_ref[sl, :]
        ms = jnp.mean(x * x, axis=-1, keepdims=True)
        h_ref[sl, :] = (x * lax.rsqrt(ms + EPS) * g_ref[...]).astype(BF16)
        return carry
    lax.fori_loop(0, x_ref.shape[0] // rows, body, 0)


def _head_rms(acc, gsum_ref, gain):
    sq = acc * acc
    hi = sq.astype(BF16)
    lo = (sq - hi.astype(F32)).astype(BF16)
    parts = []
    for s in range(acc.shape[1] // MXU_DIM):
        sl = slice(s * MXU_DIM, (s + 1) * MXU_DIM)
        parts.append(jnp.dot(hi[:, sl], gsum_ref[...], preferred_element_type=F32)
                     + jnp.dot(lo[:, sl], gsum_ref[...], preferred_element_type=F32))
    ms = jnp.concatenate(parts, axis=1) * (1.0 / FOX_HEAD_DIM)
    return acc * lax.rsqrt(ms + EPS) * gain


def _proj_kernel(x_ref, g_ref, w_ref, *refs, mode):
    h_ref = refs[-1]

    @pl.when(pl.program_id(1) == 0)
    def _():
        _rmsnorm_to_bf16(x_ref, g_ref, h_ref)

    acc = jnp.dot(h_ref[...], w_ref[...], preferred_element_type=F32)
    if mode == "plain":
        (o_ref,) = refs[:-1]
        o_ref[...] = acc.astype(o_ref.dtype)
    elif mode == "small":
        b_ref, o_ref = refs[:-1]
        lane = lax.broadcasted_iota(jnp.int32, acc.shape, 1)
        o_ref[...] = jnp.where(lane < CE_ONES, _log_sigmoid(acc + b_ref[...]), acc)
    elif mode == "qnorm":
        gsum_ref, gain_ref, o_ref = refs[:-1]
        o_ref[...] = (_head_rms(acc, gsum_ref, gain_ref[...]) * (FOX_HEAD_DIM ** -0.5 * LOG2E)).astype(BF16)
    elif mode == "knorm":
        gsum_ref, gain_ref, o32_ref, o16_ref = refs[:-1]
        y = _head_rms(acc, gsum_ref, gain_ref[...])
        o32_ref[...] = y
        o16_ref[...] = y.astype(BF16)
    elif mode == "dual":
        o32_ref, o16_ref = refs[:-1]
        o32_ref[...] = acc
        o16_ref[...] = acc.astype(BF16)
    else:
        o32_ref, o16t_ref = refs[:-1]
        o32_ref[...] = acc
        o16t_ref[...] = acc.T.astype(BF16)


def _proj(x, g, w, mode, extras=()):
    m, n = x.shape[0], w.shape[1]
    tm = min(m, 1024)
    tn = min(n, 1024 if mode == "plain" else 512)
    grid = (m // tm, n // tn)
    in_specs = [
        pl.BlockSpec((tm, D_MODEL), lambda i, j: (i, 0)),
        pl.BlockSpec((1, D_MODEL), lambda i, j: (0, 0)),
        pl.BlockSpec((D_MODEL, tn), lambda i, j: (0, j)),
    ]
    tile = pl.BlockSpec((tm, tn), lambda i, j: (i, j))
    row = pl.BlockSpec((1, tn), lambda i, j: (0, j))
    if mode == "plain":
        out_shape, out_specs = jax.ShapeDtypeStruct((m, n), BF16), tile
    elif mode == "small":
        in_specs += [row]
        out_shape, out_specs = jax.ShapeDtypeStruct((m, n), F32), tile
    elif mode == "qnorm":
        in_specs += [pl.BlockSpec((MXU_DIM, MXU_DIM), lambda i, j: (0, 0)), row]
        out_shape, out_specs = jax.ShapeDtypeStruct((m, n), BF16), tile
    elif mode == "dual_t":
        out_shape = (jax.ShapeDtypeStruct((m, n), F32), jax.ShapeDtypeStruct((n, m), BF16))
        out_specs = (tile, pl.BlockSpec((tn, tm), lambda i, j: (j, i)))
    else:
        if mode == "knorm":
            in_specs += [pl.BlockSpec((MXU_DIM, MXU_DIM), lambda i, j: (0, 0)), row]
        out_shape = (jax.ShapeDtypeStruct((m, n), F32), jax.ShapeDtypeStruct((m, n), BF16))
        out_specs = (tile, tile)
    return pl.pallas_call(
        functools.partial(_proj_kernel, mode=mode),
        grid=grid, in_specs=in_specs, out_specs=out_specs, out_shape=out_shape,
        scratch_shapes=[pltpu.VMEM((tm, D_MODEL), BF16)],
        compiler_params=_params("parallel", "arbitrary"),
        name="proj_" + mode,
    )(x, g, w, *extras)


def _encode_cumsum(cs):
    c1, c2, c3 = _split3(cs * LOG2E)
    lane = lax.broadcasted_iota(jnp.int32, cs.shape, 1)
    one = jnp.ones_like(c1)
    return jnp.where(lane < FOX_HEADS, c1,
                     jnp.where(lane < 2 * FOX_HEADS, c2, jnp.where(lane < CE_ONES, c3, one)))


def _ce_prompt_kernel(s_ref, ce_ref, carry_ref):
    @pl.when(pl.program_id(0) == 0)
    def _():
        carry_ref[...] = jnp.zeros_like(carry_ref)

    t = s_ref.shape[0]
    cs = _cumsum_rows(_lower_tri(t), s_ref[...]) + carry_ref[...]
    carry_ref[...] = cs[t - 1:t, :]
    ce_ref[...] = _encode_cumsum(cs)


def _ce_prompt(small):
    m = small.shape[0]
    t = MXU_DIM
    return pl.pallas_call(
        _ce_prompt_kernel,
        grid=(m // t,),
        in_specs=[pl.BlockSpec((t, LANES), lambda i: (i, 0))],
        out_specs=pl.BlockSpec((t, LANES), lambda i: (i, 0)),
        out_shape=jax.ShapeDtypeStruct((m, LANES), BF16),
        scratch_shapes=[pltpu.VMEM((1, LANES), F32)],
        compiler_params=_params("arbitrary"),
        name="ce_prompt",
    )(small)


def _ce_sample_kernel(c_ref, s_ref, cec_ref, cen_ref):
    t = MXU_DIM
    tri = _lower_tri(t)
    carry = jnp.zeros((1, LANES), F32)
    for b in range(c_ref.shape[1] // t):
        cs = _cumsum_rows(tri, c_ref[0, b * t:(b + 1) * t, :]) + carry
        carry = cs[t - 1:t, :]
        cec_ref[0, b * t:(b + 1) * t, :] = _encode_cumsum(cs)
    n = s_ref.shape[0]
    cs = _cumsum_rows(_lower_tri(n), s_ref[...]) + carry
    cen_ref[...] = _encode_cumsum(cs)


def _ce_sample(cache_rep, small, batch, seq):
    past = cache_rep.shape[1]
    return pl.pallas_call(
        _ce_sample_kernel,
        grid=(batch,),
        in_specs=[pl.BlockSpec((1, past, LANES), lambda b: (b, 0, 0)),
                  pl.BlockSpec((seq, LANES), lambda b: (b, 0))],
        out_specs=(pl.BlockSpec((1, past, LANES), lambda b: (b, 0, 0)),
                   pl.BlockSpec((seq, LANES), lambda b: (b, 0))),
        out_shape=(jax.ShapeDtypeStruct((batch, past, LANES), BF16),
                   jax.ShapeDtypeStruct((batch * seq, LANES), BF16)),
        compiler_params=_params("parallel"),
        name="ce_sample",
    )(cache_rep, small)


def _gla_kernel(*refs, chunk, has_state):
    if has_state:
        (q_ref, k_ref, v_ref, gg_ref, sm_ref, au_ref, ab_ref, gn_ref, s0_ref,
         o_ref, sout_ref, st_ref) = refs
    else:
        (q_ref, k_ref, v_ref, gg_ref, sm_ref, au_ref, ab_ref, gn_ref,
         o_ref, sout_ref, st_ref) = refs
    blk = pl.program_id(1)
    hk, hv = GLA_HEAD_K, GLA_HEAD_V
    tb = q_ref.shape[0]
    n_chunks, n_sub = tb // chunk, chunk // GLA_SUB
    heads = range(GLA_HEADS)

    @pl.when(blk == 0)
    def _():
        for h in heads:
            st_ref[h] = s0_ref[0, h].T if has_state else jnp.zeros((hv, hk), F32)

    a = jnp.dot(sm_ref[...].astype(BF16), au_ref[...], preferred_element_type=F32) + ab_ref[...]
    log_a = _log_sigmoid(a) * (1.0 / GLA_GATE_NORMALIZER)
    r = lax.broadcasted_iota(jnp.int32, (tb, tb), 0)
    c = lax.broadcasted_iota(jnp.int32, (tb, tb), 1)
    tri = jnp.where((r >= c) & (r // chunk == c // chunk), 1.0, 0.0).astype(BF16)
    bc = _cumsum_rows(tri, log_a)

    def rows_like(row_of, group):
        return jnp.concatenate([jnp.broadcast_to(bc[row_of(g):row_of(g) + 1, :], (group, bc.shape[1]))
                                for g in range(tb // group)], axis=0)

    last = rows_like(lambda g: (g + 1) * chunk - 1, chunk)
    anchor = rows_like(lambda g: g * GLA_SUB, GLA_SUB)
    scale = hk ** -0.5
    q = q_ref[...].astype(F32)
    k = k_ref[...].astype(F32)
    q_inter = (q * jnp.exp(bc) * scale).astype(BF16)
    k_dec = (k * jnp.exp(last - bc)).astype(BF16)
    q_sub = (q * jnp.exp(bc - anchor) * scale).astype(BF16)

    att = {}
    for ch in range(n_chunks):
        r0 = ch * chunk
        q_slots, k_slots = [], []
        for i in range(n_sub):
            lo, hi = i * GLA_SUB, (i + 1) * GLA_SUB
            k_i = (k[r0:r0 + hi] * jnp.exp(anchor[r0 + lo:r0 + lo + 1] - bc[r0:r0 + hi])).astype(BF16)
            k_slots.append(jnp.concatenate([k_i, jnp.zeros((chunk - hi, k.shape[1]), BF16)], axis=0)
                           if hi < chunk else k_i)
            pieces = [jnp.zeros((lo, q.shape[1]), BF16)] if lo else []
            pieces.append(q_sub[r0 + lo:r0 + hi])
            if hi < chunk:
                pieces.append(jnp.zeros((chunk - hi, q.shape[1]), BF16))
            q_slots.append(jnp.concatenate(pieces, axis=0) if len(pieces) > 1 else pieces[0])
        for h in heads:
            kc = slice(h * hk, (h + 1) * hk)
            q_hat = jnp.concatenate([s[:, kc] for s in q_slots], axis=1)
            k_hat = jnp.concatenate([s[:, kc] for s in k_slots], axis=1)
            att[ch, h] = lax.dot_general(q_hat, k_hat, NT_DIMS, preferred_element_type=F32)
    ri = lax.broadcasted_iota(jnp.int32, (chunk, chunk), 0)
    ci = lax.broadcasted_iota(jnp.int32, (chunk, chunk), 1)
    o_intra = {}
    for (ch, h), s in att.items():
        p = jnp.where(ci <= ri, s, 0.0).astype(BF16)
        o_intra[ch, h] = jnp.dot(p, v_ref[ch * chunk:(ch + 1) * chunk, h * hv:(h + 1) * hv],
                                 preferred_element_type=F32)

    for ch in range(n_chunks):
        rows = slice(ch * chunk, (ch + 1) * chunk)
        decay = jnp.exp(bc[(ch + 1) * chunk - 1:(ch + 1) * chunk, :])
        for h in heads:
            kc = slice(h * hk, (h + 1) * hk)
            vc = slice(h * hv, (h + 1) * hv)
            st = st_ref[h]
            o = o_intra[ch, h] + lax.dot_general(q_inter[rows, kc], st.astype(BF16), NT_DIMS,
                                                 preferred_element_type=F32)
            st_ref[h] = st * decay[:, kc] + lax.dot_general(v_ref[rows, vc], k_dec[rows, kc], TN_DIMS,
                                                            preferred_element_type=F32)
            ms = jnp.mean(o * o, axis=-1, keepdims=True)
            y = o * lax.rsqrt(ms + EPS) * gn_ref[...]
            gg = gg_ref[rows, vc].astype(F32)
            o_ref[rows, vc] = (y * (gg * jax.nn.sigmoid(gg))).astype(BF16)

    @pl.when(blk == pl.num_programs(1) - 1)
    def _():
        for h in heads:
            sout_ref[0, h] = st_ref[h].T


def _gla(pg, small, alpha_pad, alpha_b, gnorm, s0, batch, seq):
    chunk = min(GLA_CHUNK, seq)
    tb = min(seq, 4 * chunk)
    nblk = seq // tb
    hk, hv = GLA_HEAD_K, GLA_HEAD_V
    row = lambda b, t: b * nblk + t
    in_specs = [
        pl.BlockSpec((tb, GLA_DK), lambda b, t: (row(b, t), 0)),
        pl.BlockSpec((tb, GLA_DK), lambda b, t: (row(b, t), 1)),
        pl.BlockSpec((tb, GLA_DV), lambda b, t: (row(b, t), 1)),
        pl.BlockSpec((tb, GLA_DV), lambda b, t: (row(b, t), 2)),
        pl.BlockSpec((tb, LANES), lambda b, t: (row(b, t), 0)),
        pl.BlockSpec((LANES, GLA_DK), lambda b, t: (0, 0)),
        pl.BlockSpec((1, GLA_DK), lambda b, t: (0, 0)),
        pl.BlockSpec((1, hv), lambda b, t: (0, 0)),
    ]
    args = [pg, pg, pg, pg, small, alpha_pad, alpha_b, gnorm]
    state_spec = pl.BlockSpec((1, GLA_HEADS, hk, hv), lambda b, t: (b, 0, 0, 0))
    if s0 is not None:
        in_specs.append(state_spec)
        args.append(s0)
    return pl.pallas_call(
        functools.partial(_gla_kernel, chunk=chunk, has_state=s0 is not None),
        grid=(batch, nblk),
        in_specs=in_specs,
        out_specs=(pl.BlockSpec((tb, GLA_DV), lambda b, t: (row(b, t), 0)), state_spec),
        out_shape=(jax.ShapeDtypeStruct((batch * seq, GLA_DV), BF16),
                   jax.ShapeDtypeStruct((batch, GLA_HEADS, hk, hv), F32)),
        scratch_shapes=[pltpu.VMEM((GLA_HEADS, hv, hk), F32)],
        compiler_params=_params("parallel", "arbitrary"),
        name="gla",
    )(*args)


def _fox_query_operand(q, ce_q, head, half):
    lane = lax.broadcasted_iota(jnp.int32, (1, LANES), 1)
    keep = jnp.where((lane >= half * FOX_HEAD_DIM) & (lane < (half + 1) * FOX_HEAD_DIM), 1.0, 0.0)
    q_m = q * keep.astype(BF16)
    r = lax.broadcasted_iota(jnp.int32, (LANES, LANES), 0)
    c = lax.broadcasted_iota(jnp.int32, (LANES, LANES), 1)
    pick = jnp.zeros((LANES, LANES), F32)
    for part in range(CE_PARTS):
        pick = jnp.where((r == head + part * FOX_HEADS) & (c == CE_ONES + part), 1.0, pick)
    enc = jnp.dot(ce_q, pick.astype(BF16), preferred_element_type=F32)
    minus = jnp.zeros((1, LANES), F32)
    for part in range(CE_PARTS):
        minus = jnp.where(lane == head + part * FOX_HEADS, -1.0, minus)
    return jnp.concatenate([q_m, (enc + minus).astype(BF16)], axis=1)


def _fox_prompt_kernel(first_ref, q_ref, ceq_ref, k_ref, cek_ref, vt_ref, o_ref, st_ref, *, tq, tk):
    pair = pl.program_id(0)
    qi = pl.program_id(1)
    first = first_ref[pair * pl.num_programs(1) + qi]
    qt = FOX_QUERY_TILE
    nt = tq // qt
    tail_blocks = tq // tk
    full_blocks = qi * tail_blocks
    chains = [(half, r) for half in range(2) for r in range(nt)]
    everyone = list(range(len(chains)))
    needs = lambda t: [n for n in everyone if t < tail_blocks and (chains[n][1] + 1) * qt > t * tk]
    acc_rows = FOX_HEAD_DIM + FOX_SUM_ROWS

    def query_operands():
        ops = [_fox_query_operand(q_ref[r * qt:(r + 1) * qt, :], ceq_ref[r * qt:(r + 1) * qt, :],
                                  2 * pair + half, half) for half, r in chains]
        return [op.astype(F32).T.astype(BF16) for op in ops]

    def key_operand(block):
        ks = pl.multiple_of(block * tk, tk)
        return jnp.concatenate([k_ref[pl.ds(ks, tk), :], cek_ref[pl.ds(ks, tk), :]], axis=1)

    def value_operands(block):
        ks = pl.multiple_of(block * tk, tk)
        ones = jnp.ones((FOX_SUM_ROWS, tk), BF16)
        return [jnp.concatenate([vt_ref[h * FOX_HEAD_DIM:(h + 1) * FOX_HEAD_DIM, pl.ds(ks, tk)], ones], axis=0)
                for h in range(2)]

    def causal(st, r, tail_block):
        key = lax.broadcasted_iota(jnp.int32, st.shape, 0) + (tail_block * tk - r * qt)
        qry = lax.broadcasted_iota(jnp.int32, st.shape, 1)
        return jnp.where(key <= qry, st, -jnp.inf)

    def write_out(accs):
        for r in range(nt):
            outs = []
            for half in range(2):
                acc = accs[chains.index((half, r))]
                outs.append(acc[:FOX_HEAD_DIM] / acc[FOX_HEAD_DIM:FOX_HEAD_DIM + 1])
            o_ref[r * qt:(r + 1) * qt, :] = jnp.concatenate(outs, axis=0).T.astype(BF16)

    q_ops = query_operands()

    def step(block, slot, carries, attend, score_next, tail_block=None):
        carries = list(carries)
        if attend:
            vts = value_operands(block)
        if score_next:
            k_op = key_operand(block + 1)
        for n in everyone:
            if n in score_next:
                st_ref[1 - slot, n] = jnp.dot(k_op, q_ops[n], preferred_element_type=F32)
            if n in attend:
                half, r = chains[n]
                m, acc = carries[n]
                st = st_ref[slot, n]
                if tail_block is not None:
                    st = causal(st, r, tail_block)
                m_new = jnp.maximum(m, jnp.max(st, axis=0, keepdims=True))
                p = jnp.exp2(st - m_new).astype(BF16)
                acc = jnp.exp2(m - m_new) * acc + jnp.dot(vts[half], p, preferred_element_type=F32)
                carries[n] = (m_new, acc)
        return tuple(carries)

    def body(jj, carries):
        for u in range(FOX_LOOP_UNROLL):
            carries = step(FOX_LOOP_UNROLL * jj + u, u % 2, carries, everyone, everyone)
        return carries

    init = tuple((jnp.full((1, qt), -jnp.inf, F32), jnp.zeros((acc_rows, qt), F32)) for _ in chains)
    carries = step(first - 1, 1, init, [], everyone)
    carries = lax.fori_loop(first // FOX_LOOP_UNROLL, full_blocks // FOX_LOOP_UNROLL, body, carries)
    for t in range(tail_blocks):
        carries = step(full_blocks + t, t % 2, carries, needs(t), needs(t + 1), tail_block=t)
    write_out([acc for _, acc in carries])


def _fox_first_blocks(logf, gq, gk, tq, tk):
    m = logf.shape[0]
    bound = FOX_HEAD_DIM ** 0.5 * jnp.max(jnp.abs(gq)) * jnp.max(jnp.abs(gk)) * LOG2E * 1.02 + 1.0
    c = jnp.cumsum(logf, axis=0) * LOG2E
    c_q = c[0::tq]
    c_k = c[tk - 1::tk]
    dead = 2.0 * bound + (c_q[:, None, :] - c_k[None, :, :]) < FOX_UNDERFLOW_LOG2
    dead &= (jnp.arange(m // tk)[None, :, None] < (jnp.arange(m // tq) * (tq // tk))[:, None, None])
    lead = jnp.sum(jnp.cumprod(dead.astype(jnp.int32), axis=1), axis=1)
    lead = jnp.min(lead.reshape(m // tq, FOX_HEADS // 2, 2), axis=-1)
    lead = lead // FOX_LOOP_UNROLL * FOX_LOOP_UNROLL
    return lead.T.reshape(-1).astype(jnp.int32)


def _fox_prompt(q16, k16, vt16, ce, logf, gq, gk):
    m = q16.shape[0]
    tq = min(m, FOX_QUERY_BLOCK)
    tk = min(tq // 2, FOX_KEY_BLOCK)
    assert (tq // tk) % FOX_LOOP_UNROLL == 0 and tk % FOX_QUERY_TILE == 0
    pairs = FOX_HEADS // 2
    chains = 2 * (tq // FOX_QUERY_TILE)
    return pl.pallas_call(
        functools.partial(_fox_prompt_kernel, tq=tq, tk=tk),
        grid=(pairs, m // tq),
        in_specs=[
            pl.BlockSpec(memory_space=pltpu.SMEM),
            pl.BlockSpec((tq, LANES), lambda p, i: (i, p)),
            pl.BlockSpec((tq, LANES), lambda p, i: (i, 0)),
            pl.BlockSpec((m, LANES), lambda p, i: (0, p)),
            pl.BlockSpec((m, LANES), lambda p, i: (0, 0)),
            pl.BlockSpec((LANES, m), lambda p, i: (p, 0)),
        ],
        out_specs=pl.BlockSpec((tq, LANES), lambda p, i: (i, p)),
        out_shape=jax.ShapeDtypeStruct((m, FOX_DIM), BF16),
        scratch_shapes=[pltpu.VMEM((2, chains, tk, FOX_QUERY_TILE), F32)],
        compiler_params=_params("parallel", "arbitrary"),
        name="fox_prompt",
    )(_fox_first_blocks(logf, gq, gk, tq, tk), q16, ce, k16, ce, vt16)


def _fox_sample_kernel(q_ref, kn_ref, vn_ref, cen_ref, kc_ref, vc_ref, cec_ref, o_ref, k_all, v_all):
    pair = pl.program_id(1)
    past = kc_ref.shape[0]
    seq = q_ref.shape[0]
    total = k_all.shape[0]
    k_all[0:past, 0:LANES] = kc_ref[...].astype(BF16)
    k_all[0:past, LANES:2 * LANES] = cec_ref[0]
    k_all[past:past + seq, 0:LANES] = kn_ref[...]
    k_all[past:past + seq, LANES:2 * LANES] = cen_ref[...]
    k_all[past + seq:total, :] = jnp.zeros((total - past - seq, 2 * LANES), BF16)
    v_all[0:past, :] = vc_ref[...].astype(BF16)
    v_all[past:past + seq, :] = vn_ref[...]
    v_all[past + seq:total, :] = jnp.zeros((total - past - seq, LANES), BF16)

    q_op = jnp.concatenate([_fox_query_operand(q_ref[...], cen_ref[...], 2 * pair + half, half)
                            for half in range(2)], axis=0)
    s = lax.dot_general(q_op, k_all[...], NT_DIMS, preferred_element_type=F32)
    qr = lax.broadcasted_iota(jnp.int32, s.shape, 0)
    qr = jnp.where(qr >= seq, qr - seq, qr)
    kc = lax.broadcasted_iota(jnp.int32, s.shape, 1)
    s = jnp.where(kc <= qr + past, s, -jnp.inf)
    p = jnp.exp2(s - jnp.max(s, axis=1, keepdims=True))
    l = jnp.sum(p, axis=1, keepdims=True)
    o = jnp.dot(p.astype(BF16), v_all[...], preferred_element_type=F32) / l
    lane = lax.broadcasted_iota(jnp.int32, (seq, LANES), 1)
    o_ref[...] = jnp.where(lane < FOX_HEAD_DIM, o[:seq], o[seq:]).astype(BF16)


def _fox_sample(q16, k16, v16, ce_new, cache_k, cache_v, layer, ce_cache, batch, seq):
    past = cache_k.shape[2]
    total = -(-(past + seq) // LANES) * LANES
    pairs = FOX_HEADS // 2
    new = pl.BlockSpec((seq, LANES), lambda b, p: (b, p))
    cache = pl.BlockSpec((None, None, past, LANES), lambda b, p: (layer, b, 0, p))
    return pl.pallas_call(
        _fox_sample_kernel,
        grid=(batch, pairs),
        in_specs=[new, new, new, pl.BlockSpec((seq, LANES), lambda b, p: (b, 0)),
                  cache, cache, pl.BlockSpec((1, past, LANES), lambda b, p: (b, 0, 0))],
        out_specs=new,
        out_shape=jax.ShapeDtypeStruct((batch * seq, FOX_DIM), BF16),
        scratch_shapes=[pltpu.VMEM((total, 2 * LANES), BF16), pltpu.VMEM((total, LANES), BF16)],
        compiler_params=_params("parallel", "arbitrary"),
        name="fox_sample",
    )(q16, k16, v16, ce_new, cache_k, cache_v, ce_cache)


def _merge_kernel(oa_ref, wa_ref, ob_ref, wb_ref, ga_ref, gb_ref, o_ref):
    a = jnp.dot(oa_ref[...], wa_ref[...], preferred_element_type=F32)
    b = jnp.dot(ob_ref[...], wb_ref[...], preferred_element_type=F32)
    ga = jax.nn.sigmoid(ga_ref[...].astype(F32))
    gb = jax.nn.sigmoid(gb_ref[...].astype(F32))
    o_ref[...] = (ga * a + gb * b).astype(BF16)


def _merge(o_gla, w_a, o_fox, w_b, gates):
    m = o_gla.shape[0]
    tm, tn = min(m, 1024), 1024
    nb = D_MODEL // tn
    act = pl.BlockSpec((tm, D_MODEL), lambda i, j: (i, 0))
    wgt = pl.BlockSpec((D_MODEL, tn), lambda i, j: (0, j))
    return pl.pallas_call(
        _merge_kernel,
        grid=(m // tm, nb),
        in_specs=[act, wgt, act, wgt,
                  pl.BlockSpec((tm, tn), lambda i, j: (i, j)),
                  pl.BlockSpec((tm, tn), lambda i, j: (i, j + nb))],
        out_specs=pl.BlockSpec((tm, tn), lambda i, j: (i, j)),
        out_shape=jax.ShapeDtypeStruct((m, D_MODEL), BF16),
        compiler_params=_params("parallel", "arbitrary"),
        name="merge",
    )(o_gla, w_a, o_fox, w_b, gates, gates)


def _out_proj_kernel(a_ref, w_ref, x_ref, o_ref):
    o_ref[...] = x_ref[...] + jnp.dot(a_ref[...], w_ref[...], preferred_element_type=F32)


def _out_proj(merged, w, x):
    m = x.shape[0]
    tm, tn = min(m, 1024), 1024
    return pl.pallas_call(
        _out_proj_kernel,
        grid=(m // tm, D_MODEL // tn),
        in_specs=[pl.BlockSpec((tm, D_MODEL), lambda i, j: (i, 0)),
                  pl.BlockSpec((D_MODEL, tn), lambda i, j: (0, j)),
                  pl.BlockSpec((tm, tn), lambda i, j: (i, j))],
        out_specs=pl.BlockSpec((tm, tn), lambda i, j: (i, j)),
        out_shape=jax.ShapeDtypeStruct((m, D_MODEL), F32),
        compiler_params=_params("parallel", "arbitrary"),
        name="out_proj",
    )(merged, w, x)


def _mlp_kernel(x_ref, g_ref, wu_ref, wd_ref, o_ref, h_ref):
    @pl.when(pl.program_id(1) == 0)
    def _():
        _rmsnorm_to_bf16(x_ref, g_ref, h_ref)
        o_ref[...] = x_ref[...]

    u = jnp.maximum(jnp.dot(h_ref[...], wu_ref[...], preferred_element_type=F32), 0.0)
    o_ref[...] += jnp.dot((u * u).astype(BF16), wd_ref[...], preferred_element_type=F32)


def _mlp(x, g, w_up, w_down):
    m = x.shape[0]
    tm, tf = min(m, 512), 512
    return pl.pallas_call(
        _mlp_kernel,
        grid=(m // tm, D_FF // tf),
        in_specs=[pl.BlockSpec((tm, D_MODEL), lambda i, f: (i, 0)),
                  pl.BlockSpec((1, D_MODEL), lambda i, f: (0, 0)),
                  pl.BlockSpec((D_MODEL, tf), lambda i, f: (0, f)),
                  pl.BlockSpec((tf, D_MODEL), lambda i, f: (f, 0))],
        out_specs=pl.BlockSpec((tm, D_MODEL), lambda i, f: (i, 0)),
        out_shape=jax.ShapeDtypeStruct((m, D_MODEL), F32),
        scratch_shapes=[pltpu.VMEM((tm, D_MODEL), BF16)],
        compiler_params=_params("parallel", "arbitrary"),
        name="mlp",
    )(x, g, w_up, w_down)


def _prepare_layer(l, norm_mix_g, w_in, gla_alpha_up, gla_alpha_b, gla_norm_g, fox_f_b, fox_qnorm_g,
                   fox_knorm_g, w_br_gla, w_br_fox, w_out, norm_mlp_g, w_up, w_down):
    w = w_in[l]
    o = 0
    cols = {}
    for name, size in (("gla", 2 * GLA_DK + 2 * GLA_DV), ("low", GLA_LOWRANK), ("fq", FOX_DIM),
                       ("fk", FOX_DIM), ("fv", FOX_DIM), ("ff", FOX_HEADS), ("gate", 2 * D_MODEL)):
        cols[name] = w[:, o:o + size]
        o += size
    pad = LANES - CE_ONES - GLA_LOWRANK
    w_small = jnp.concatenate([cols["ff"]] * CE_PARTS + [cols["low"], jnp.zeros((D_MODEL, pad), F32)], axis=1)
    b_small = jnp.concatenate([fox_f_b[l]] * CE_PARTS + [jnp.zeros((LANES - CE_ONES,), F32)])[None, :]
    alpha_pad = jnp.zeros((LANES, GLA_DK), F32).at[LOWRANK_LANE:LOWRANK_LANE + GLA_LOWRANK].set(gla_alpha_up[l])
    return dict(
        norm_mix_g=norm_mix_g[l][None, :], norm_mlp_g=norm_mlp_g[l][None, :],
        w_gla=cols["gla"].astype(BF16), w_small=w_small.astype(BF16), b_small=b_small,
        w_fq=cols["fq"].astype(BF16), w_fk=cols["fk"].astype(BF16), w_fv=cols["fv"].astype(BF16),
        w_gate=cols["gate"].astype(BF16),
        alpha_pad=alpha_pad.astype(BF16), alpha_b=gla_alpha_b[l][None, :], gla_norm_g=gla_norm_g[l][None, :],
        gq=jnp.tile(fox_qnorm_g[l], FOX_HEADS)[None, :], gk=jnp.tile(fox_knorm_g[l], FOX_HEADS)[None, :],        w_br_gla=w_br_gla[l].astype(BF16), w_br_fox=w_br_fox[l].astype(BF16), w_out=w_out[l].astype(BF16),
        w_up=w_up[l].astype(BF16), w_down=w_down[l].astype(BF16),
    )


def _group_sum_matrix():
    r = lax.broadcasted_iota(jnp.int32, (MXU_DIM, MXU_DIM), 0) // FOX_HEAD_DIM
    c = lax.broadcasted_iota(jnp.int32, (MXU_DIM, MXU_DIM), 1) // FOX_HEAD_DIM
    return (r == c).astype(BF16)


def _layer(x, batch, seq, cache, p, gsum):
    g = p["norm_mix_g"]
    pg = _proj(x, g, p["w_gla"], "plain")
    small = _proj(x, g, p["w_small"], "small", (p["b_small"],))
    q16 = _proj(x, g, p["w_fq"], "qnorm", (gsum, p["gq"]))
    k32, k16 = _proj(x, g, p["w_fk"], "knorm", (gsum, p["gk"]))
    v32, v16 = _proj(x, g, p["w_fv"], "dual_t" if cache is None else "dual")
    gates = _proj(x, g, p["w_gate"], "plain")

    if cache is None:
        assert batch == 1
        o_gla, s_new = _gla(pg, small, p["alpha_pad"], p["alpha_b"], p["gla_norm_g"], None, batch, seq)
        ce = _ce_prompt(small)
        o_fox = _fox_prompt(q16, k16, v16, ce, small[:, :FOX_HEADS], p["gq"], p["gk"])
    else:
        cache_k, cache_v, cache_logf, s0, layer = cache
        depth, past = cache_k.shape[0], cache_k.shape[2]
        o_gla, s_new = _gla(pg, small, p["alpha_pad"], p["alpha_b"], p["gla_norm_g"], s0, batch, seq)
        rep = jnp.concatenate([cache_logf] * CE_PARTS + [jnp.zeros((batch, past, LANES - CE_ONES), F32)], axis=-1)
        ce_cache, ce_new = _ce_sample(rep, small, batch, seq)
        o_fox = _fox_sample(q16, k16, v16, ce_new, cache_k.reshape(depth, batch, past, FOX_DIM),
                            cache_v.reshape(depth, batch, past, FOX_DIM), layer, ce_cache, batch, seq)

    merged = _merge(o_gla, p["w_br_gla"], o_fox, p["w_br_fox"], gates)
    x = _out_proj(merged, p["w_out"], x)
    x = _mlp(x, p["norm_mlp_g"], p["w_up"], p["w_down"])
    k_out = k32.reshape(batch, seq, FOX_HEADS, FOX_HEAD_DIM)
    v_out = v32.reshape(batch, seq, FOX_HEADS, FOX_HEAD_DIM)
    logf = small[:, :FOX_HEADS].reshape(batch, seq, FOX_HEADS)
    return x, k_out, v_out, logf, s_new


def kernel(x_prompt, x_sample, cache_fox_k, cache_fox_v, cache_fox_logf, state_gla, norm_mix_g, w_in,
           gla_alpha_up, gla_alpha_b, gla_norm_g, fox_f_b, fox_qnorm_g, fox_knorm_g, w_br_gla, w_br_fox,
           w_out, norm_mlp_g, w_up, w_down):
    bp, lp, _ = x_prompt.shape
    bs, ls, _ = x_sample.shape
    xp = x_prompt.reshape(bp * lp, D_MODEL)
    xs = x_sample.reshape(bs * ls, D_MODEL)
    gsum = _group_sum_matrix()
    outs_p, outs_s = [], []
    for l in range(w_in.shape[0]):
        p = _prepare_layer(l, norm_mix_g, w_in, gla_alpha_up, gla_alpha_b, gla_norm_g, fox_f_b, fox_qnorm_g,
                           fox_knorm_g, w_br_gla, w_br_fox, w_out, norm_mlp_g, w_up, w_down)
        xp, *rest_p = _layer(xp, bp, lp, None, p, gsum)
        xs, *rest_s = _layer(xs, bs, ls, (cache_fox_k, cache_fox_v, cache_fox_logf[l], state_gla[l], l), p, gsum)
        outs_p.append(rest_p)
        outs_s.append(rest_s)
    stack = lambda outs, i: jnp.stack([o[i] for o in outs])
    return (xp.reshape(bp, lp, D_MODEL), xs.reshape(bs, ls, D_MODEL),
            stack(outs_p, 0), stack(outs_p, 1), stack(outs_p, 2), stack(outs_p, 3),
            stack(outs_s, 0), stack(outs_s, 1), stack(outs_s, 2), stack(outs_s, 3))
```

```python
import functools

import jax
import jax.numpy as jnp
from jax import lax
from jax.experimental import pallas as pl
from jax.experimental.pallas import tpu as pltpu

F32 = jnp.float32
BF16 = jnp.bfloat16

D_MODEL = 2048
DEPTH = 4
GLA_HEADS = 4
GLA_HEAD_K = 256
GLA_HEAD_V = 512
GLA_DK = GLA_HEADS * GLA_HEAD_K
GLA_DV = GLA_HEADS * GLA_HEAD_V
GLA_LOWRANK = 16
GLA_GATE_NORMALIZER = 16.0
GLA_CHUNK = 64
GLA_SUB = 16
FOX_HEADS = 32
FOX_HEAD_DIM = 64
FOX_DIM = FOX_HEADS * FOX_HEAD_DIM
D_FF = 4 * D_MODEL
EPS = 1e-6

LANES = 128
MXU_DIM = 256
VMEM_LIMIT_BYTES = 56 * 1024 * 1024

CE_PARTS = 3
CE_ONES = CE_PARTS * FOX_HEADS
LOWRANK_LANE = CE_ONES

FOX_QUERY_TILE = MXU_DIM
FOX_QUERY_BLOCK = 1024
FOX_KEY_BLOCK = 512
FOX_SUM_ROWS = 16
LOG2E = 1.4426950408889634
FOX_UNDERFLOW_LOG2 = -160.0
FOX_LOOP_UNROLL = 2

NT_DIMS = (((1,), (1,)), ((), ()))
TN_DIMS = (((0,), (0,)), ((), ()))


def _params(*semantics):
    return pltpu.CompilerParams(dimension_semantics=semantics, vmem_limit_bytes=VMEM_LIMIT_BYTES)


def _log_sigmoid(x):
    return jnp.minimum(x, 0.0) - jnp.log(1.0 + jnp.exp(-jnp.abs(x)))


def _split3(x):
    p1 = x.astype(BF16)
    r1 = x - p1.astype(F32)
    p2 = r1.astype(BF16)
    p3 = (r1 - p2.astype(F32)).astype(BF16)
    return p1, p2, p3


def _lower_tri(n):
    r = lax.broadcasted_iota(jnp.int32, (n, n), 0)
    c = lax.broadcasted_iota(jnp.int32, (n, n), 1)
    return jnp.where(r >= c, 1.0, 0.0).astype(BF16)


def _cumsum_rows(tri, x):
    p1, p2, p3 = _split3(x)
    return (jnp.dot(tri, p1, preferred_element_type=F32)
            + jnp.dot(tri, p2, preferred_element_type=F32)
            + jnp.dot(tri, p3, preferred_element_type=F32))


def _rmsnorm_to_bf16(x_ref, g_ref, h_ref, rows=128):
    def body(r, carry):
        sl = pl.ds(pl.multiple_of(r * rows, rows), rows)
        x = x_ref[sl, :]
        ms = jnp.mean(x * x, axis=-1, keepdims=True)
        h_ref[sl, :] = (x * lax.rsqrt(ms + EPS) * g_ref[...]).astype(BF16)
        return carry
    lax.fori_loop(0, x_ref.shape[0] // rows, body, 0)


def _head_rms(acc, gsum_ref, gain):
    sq = acc * acc
    hi = sq.astype(BF16)
    lo = (sq - hi.astype(F32)).astype(BF16)
    parts = []
    for s in range(acc.shape[1] // MXU_DIM):
        sl = slice(s * MXU_DIM, (s + 1) * MXU_DIM)
        parts.append(jnp.dot(hi[:, sl], gsum_ref[...], preferred_element_type=F32)
                     + jnp.dot(lo[:, sl], gsum_ref[...], preferred_element_type=F32))
    ms = jnp.concatenate(parts, axis=1) * (1.0 / FOX_HEAD_DIM)
    return acc * lax.rsqrt(ms + EPS) * gain


def _proj_kernel(x_ref, g_ref, w_ref, *refs, mode):
    h_ref = refs[-1]

    @pl.when(pl.program_id(1) == 0)
    def _():
        _rmsnorm_to_bf16(x_ref, g_ref, h_ref)

    acc = jnp.dot(h_ref[...], w_ref[...], preferred_element_type=F32)
    if mode == "plain":
        (o_ref,) = refs[:-1]
        o_ref[...] = acc.astype(o_ref.dtype)
    elif mode == "small":
        b_ref, o_ref = refs[:-1]
        lane = lax.broadcasted_iota(jnp.int32, acc.shape, 1)
        o_ref[...] = jnp.where(lane < CE_ONES, _log_sigmoid(acc + b_ref[...]), acc)
    elif mode == "qnorm":
        gsum_ref, gain_ref, o_ref = refs[:-1]
        o_ref[...] = (_head_rms(acc, gsum_ref, gain_ref[...]) * (FOX_HEAD_DIM ** -0.5 * LOG2E)).astype(BF16)
    elif mode == "knorm":
        gsum_ref, gain_ref, o32_ref, o16_ref = refs[:-1]
        y = _head_rms(acc, gsum_ref, gain_ref[...])
        o32_ref[...] = y
        o16_ref[...] = y.astype(BF16)
    elif mode == "dual":
        o32_ref, o16_ref = refs[:-1]
        o32_ref[...] = acc
        o16_ref[...] = acc.astype(BF16)
    else:
        o32_ref, o16t_ref = refs[:-1]
        o32_ref[...] = acc
        o16t_ref[...] = acc.T.astype(BF16)


def _proj(x, g, w, mode, extras=()):
    m, n = x.shape[0], w.shape[1]
    tm = min(m, 1024)
    tn = min(n, 1024 if mode == "plain" else 512)
    grid = (m // tm, n // tn)
    in_specs = [
        pl.BlockSpec((tm, D_MODEL), lambda i, j: (i, 0)),
        pl.BlockSpec((1, D_MODEL), lambda i, j: (0, 0)),
        pl.BlockSpec((D_MODEL, tn), lambda i, j: (0, j)),
    ]
    tile = pl.BlockSpec((tm, tn), lambda i, j: (i, j))
    row = pl.BlockSpec((1, tn), lambda i, j: (0, j))
    if mode == "plain":
        out_shape, out_specs = jax.ShapeDtypeStruct((m, n), BF16), tile
    elif mode == "small":
        in_specs += [row]
        out_shape, out_specs = jax.ShapeDtypeStruct((m, n), F32), tile
    elif mode == "qnorm":
        in_specs += [pl.BlockSpec((MXU_DIM, MXU_DIM), lambda i, j: (0, 0)), row]
        out_shape, out_specs = jax.ShapeDtypeStruct((m, n), BF16), tile
    elif mode == "dual_t":
        out_shape = (jax.ShapeDtypeStruct((m, n), F32), jax.ShapeDtypeStruct((n, m), BF16))
        out_specs = (tile, pl.BlockSpec((tn, tm), lambda i, j: (j, i)))
    else:
        if mode == "knorm":
            in_specs += [pl.BlockSpec((MXU_DIM, MXU_DIM), lambda i, j: (0, 0)), row]
        out_shape = (jax.ShapeDtypeStruct((m, n), F32), jax.ShapeDtypeStruct((m, n), BF16))
        out_specs = (tile, tile)
    return pl.pallas_call(
        functools.partial(_proj_kernel, mode=mode),
        grid=grid, in_specs=in_specs, out_specs=out_specs, out_shape=out_shape,
        scratch_shapes=[pltpu.VMEM((tm, D_MODEL), BF16)],
        compiler_params=_params("parallel", "arbitrary"),
        name="proj_" + mode,
    )(x, g, w, *extras)


def _encode_cumsum(cs):
    c1, c2, c3 = _split3(cs * LOG2E)
    lane = lax.broadcasted_iota(jnp.int32, cs.shape, 1)
    one = jnp.ones_like(c1)
    return jnp.where(lane < FOX_HEADS, c1,
                     jnp.where(lane < 2 * FOX_HEADS, c2, jnp.where(lane < CE_ONES, c3, one)))


def _ce_prompt_kernel(s_ref, ce_ref, carry_ref):
    @pl.when(pl.program_id(0) == 0)
    def _():
        carry_ref[...] = jnp.zeros_like(carry_ref)

    t = s_ref.shape[0]
    cs = _cumsum_rows(_lower_tri(t), s_ref[...]) + carry_ref[...]
    carry_ref[...] = cs[t - 1:t, :]
    ce_ref[...] = _encode_cumsum(cs)


def _ce_prompt(small):
    m = small.shape[0]
    t = MXU_DIM
    return pl.pallas_call(
        _ce_prompt_kernel,
        grid=(m // t,),
        in_specs=[pl.BlockSpec((t, LANES), lambda i: (i, 0))],
        out_specs=pl.BlockSpec((t, LANES), lambda i: (i, 0)),
        out_shape=jax.ShapeDtypeStruct((m, LANES), BF16),
        scratch_shapes=[pltpu.VMEM((1, LANES), F32)],
        compiler_params=_params("arbitrary"),
        name="ce_prompt",
    )(small)


def _ce_sample_kernel(c_ref, s_ref, cec_ref, cen_ref):
    t = MXU_DIM
    tri = _lower_tri(t)
    carry = jnp.zeros((1, LANES), F32)
    for b in range(c_ref.shape[1] // t):
        cs = _cumsum_rows(tri, c_ref[0, b * t:(b + 1) * t, :]) + carry
        carry = cs[t - 1:t, :]
        cec_ref[0, b * t:(b + 1) * t, :] = _encode_cumsum(cs)
    n = s_ref.shape[0]
    cs = _cumsum_rows(_lower_tri(n), s_ref[...]) + carry
    cen_ref[...] = _encode_cumsum(cs)


def _ce_sample(cache_rep, small, batch, seq):
    past = cache_rep.shape[1]
    return pl.pallas_call(
        _ce_sample_kernel,
        grid=(batch,),
        in_specs=[pl.BlockSpec((1, past, LANES), lambda b: (b, 0, 0)),
                  pl.BlockSpec((seq, LANES), lambda b: (b, 0))],
        out_specs=(pl.BlockSpec((1, past, LANES), lambda b: (b, 0, 0)),
                   pl.BlockSpec((seq, LANES), lambda b: (b, 0))),
        out_shape=(jax.ShapeDtypeStruct((batch, past, LANES), BF16),
                   jax.ShapeDtypeStruct((batch * seq, LANES), BF16)),
        compiler_params=_params("parallel"),
        name="ce_sample",
    )(cache_rep, small)


def _gla_kernel(*refs, chunk, has_state):
    if has_state:
        (q_ref, k_ref, v_ref, gg_ref, sm_ref, au_ref, ab_ref, gn_ref, s0_ref,
         o_ref, sout_ref, st_ref) = refs
    else:
        (q_ref, k_ref, v_ref, gg_ref, sm_ref, au_ref, ab_ref, gn_ref,
         o_ref, sout_ref, st_ref) = refs
    blk = pl.program_id(1)
    hk, hv = GLA_HEAD_K, GLA_HEAD_V
    tb = q_ref.shape[0]
    n_chunks, n_sub = tb // chunk, chunk // GLA_SUB
    heads = range(GLA_HEADS)

    @pl.when(blk == 0)
    def _():
        for h in heads:
            st_ref[h] = s0_ref[0, h].T if has_state else jnp.zeros((hv, hk), F32)

    a = jnp.dot(sm_ref[...].astype(BF16), au_ref[...], preferred_element_type=F32) + ab_ref[...]
    log_a = _log_sigmoid(a) * (1.0 / GLA_GATE_NORMALIZER)
    r = lax.broadcasted_iota(jnp.int32, (tb, tb), 0)
    c = lax.broadcasted_iota(jnp.int32, (tb, tb), 1)
    tri = jnp.where((r >= c) & (r // chunk == c // chunk), 1.0, 0.0).astype(BF16)
    bc = _cumsum_rows(tri, log_a)

    def rows_like(row_of, group):
        return jnp.concatenate([jnp.broadcast_to(bc[row_of(g):row_of(g) + 1, :], (group, bc.shape[1]))
                                for g in range(tb // group)], axis=0)

    last = rows_like(lambda g: (g + 1) * chunk - 1, chunk)
    anchor = rows_like(lambda g: g * GLA_SUB, GLA_SUB)
    scale = hk ** -0.5
    q = q_ref[...].astype(F32)
    k = k_ref[...].astype(F32)
    q_inter = (q * jnp.exp(bc) * scale).astype(BF16)
    k_dec = (k * jnp.exp(last - bc)).astype(BF16)
    q_sub = (q * jnp.exp(bc - anchor) * scale).astype(BF16)

    att = {}
    for ch in range(n_chunks):
        r0 = ch * chunk
        q_slots, k_slots = [], []
        for i in range(n_sub):
            lo, hi = i * GLA_SUB, (i + 1) * GLA_SUB
            k_i = (k[r0:r0 + hi] * jnp.exp(anchor[r0 + lo:r0 + lo + 1] - bc[r0:r0 + hi])).astype(BF16)
            k_slots.append(jnp.concatenate([k_i, jnp.zeros((chunk - hi, k.shape[1]), BF16)], axis=0)
                           if hi < chunk else k_i)
            pieces = [jnp.zeros((lo, q.shape[1]), BF16)] if lo else []
            pieces.append(q_sub[r0 + lo:r0 + hi])
            if hi < chunk:
                pieces.append(jnp.zeros((chunk - hi, q.shape[1]), BF16))
            q_slots.append(jnp.concatenate(pieces, axis=0) if len(pieces) > 1 else pieces[0])
        for h in heads:
            kc = slice(h * hk, (h + 1) * hk)
            q_hat = jnp.concatenate([s[:, kc] for s in q_slots], axis=1)
            k_hat = jnp.concatenate([s[:, kc] for s in k_slots], axis=1)
            att[ch, h] = lax.dot_general(q_hat, k_hat, NT_DIMS, preferred_element_type=F32)
    ri = lax.broadcasted_iota(jnp.int32, (chunk, chunk), 0)
    ci = lax.broadcasted_iota(jnp.int32, (chunk, chunk), 1)
    o_intra = {}
    for (ch, h), s in att.items():
        p = jnp.where(ci <= ri, s, 0.0).astype(BF16)
        o_intra[ch, h] = jnp.dot(p, v_ref[ch * chunk:(ch + 1) * chunk, h * hv:(h + 1) * hv],
                                 preferred_element_type=F32)

    for ch in range(n_chunks):
        rows = slice(ch * chunk, (ch + 1) * chunk)
        decay = jnp.exp(bc[(ch + 1) * chunk - 1:(ch + 1) * chunk, :])
        for h in heads:
            kc = slice(h * hk, (h + 1) * hk)
            vc = slice(h * hv, (h + 1) * hv)
            st = st_ref[h]
            o = o_intra[ch, h] + lax.dot_general(q_inter[rows, kc], st.astype(BF16), NT_DIMS,
                                                 preferred_element_type=F32)
            st_ref[h] = st * decay[:, kc] + lax.dot_general(v_ref[rows, vc], k_dec[rows, kc], TN_DIMS,
                                                            preferred_element_type=F32)
            ms = jnp.mean(o * o, axis=-1, keepdims=True)
            y = o * lax.rsqrt(ms + EPS) * gn_ref[...]
            gg = gg_ref[rows, vc].astype(F32)
            o_ref[rows, vc] = (y * (gg * jax.nn.sigmoid(gg))).astype(BF16)

    @pl.when(blk == pl.num_programs(1) - 1)
    def _():
        for h in heads:
            sout_ref[0, h] = st_ref[h].T


def _gla(pg, small, alpha_pad, alpha_b, gnorm, s0, batch, seq):
    chunk = min(GLA_CHUNK, seq)
    tb = min(seq, 4 * chunk)
    nblk = seq // tb
    hk, hv = GLA_HEAD_K, GLA_HEAD_V
    row = lambda b, t: b * nblk + t
    in_specs = [
        pl.BlockSpec((tb, GLA_DK), lambda b, t: (row(b, t), 0)),
        pl.BlockSpec((tb, GLA_DK), lambda b, t: (row(b, t), 1)),
        pl.BlockSpec((tb, GLA_DV), lambda b, t: (row(b, t), 1)),
        pl.BlockSpec((tb, GLA_DV), lambda b, t: (row(b, t), 2)),
        pl.BlockSpec((tb, LANES), lambda b, t: (row(b, t), 0)),
        pl.BlockSpec((LANES, GLA_DK), lambda b, t: (0, 0)),
        pl.BlockSpec((1, GLA_DK), lambda b, t: (0, 0)),
        pl.BlockSpec((1, hv), lambda b, t: (0, 0)),
    ]
    args = [pg, pg, pg, pg, small, alpha_pad, alpha_b, gnorm]
    state_spec = pl.BlockSpec((1, GLA_HEADS, hk, hv), lambda b, t: (b, 0, 0, 0))
    if s0 is not None:
        in_specs.append(state_spec)
        args.append(s0)
    return pl.pallas_call(
        functools.partial(_gla_kernel, chunk=chunk, has_state=s0 is not None),
        grid=(batch, nblk),
        in_specs=in_specs,
        out_specs=(pl.BlockSpec((tb, GLA_DV), lambda b, t: (row(b, t), 0)), state_spec),
        out_shape=(jax.ShapeDtypeStruct((batch * seq, GLA_DV), BF16),
                   jax.ShapeDtypeStruct((batch, GLA_HEADS, hk, hv), F32)),
        scratch_shapes=[pltpu.VMEM((GLA_HEADS, hv, hk), F32)],
        compiler_params=_params("parallel", "arbitrary"),
        name="gla",
    )(*args)


def _fox_query_operand(q, ce_q, head, half):
    lane = lax.broadcasted_iota(jnp.int32, (1, LANES), 1)
    keep = jnp.where((lane >= half * FOX_HEAD_DIM) & (lane < (half + 1) * FOX_HEAD_DIM), 1.0, 0.0)
    q_m = q * keep.astype(BF16)
    r = lax.broadcasted_iota(jnp.int32, (LANES, LANES), 0)
    c = lax.broadcasted_iota(jnp.int32, (LANES, LANES), 1)
    pick = jnp.zeros((LANES, LANES), F32)
    for part in range(CE_PARTS):
        pick = jnp.where((r == head + part * FOX_HEADS) & (c == CE_ONES + part), 1.0, pick)
    enc = jnp.dot(ce_q, pick.astype(BF16), preferred_element_type=F32)
    minus = jnp.zeros((1, LANES), F32)
    for part in range(CE_PARTS):
        minus = jnp.where(lane == head + part * FOX_HEADS, -1.0, minus)
    return jnp.concatenate([q_m, (enc + minus).astype(BF16)], axis=1)


def _fox_prompt_kernel(first_ref, q_ref, ceq_ref, k_ref, cek_ref, vt_ref, o_ref, st_ref, *, tq, tk):
    pair = pl.program_id(0)
    qi = pl.program_id(1)
    first = [first_ref[(2 * pair + half) * pl.num_programs(1) + qi] for half in range(2)]
    qt = FOX_QUERY_TILE
    nt = tq // qt
    tail_blocks = tq // tk
    full_blocks = qi * tail_blocks
    chains = [(half, r) for half in range(2) for r in range(nt)]
    everyone = list(range(len(chains)))
    needs = lambda t: [n for n in everyone if t < tail_blocks and (chains[n][1] + 1) * qt > t * tk]
    acc_rows = FOX_HEAD_DIM + FOX_SUM_ROWS

    def query_operands():
        ops = [_fox_query_operand(q_ref[r * qt:(r + 1) * qt, :], ceq_ref[r * qt:(r + 1) * qt, :],
                                  2 * pair + half, half) for half, r in chains]
        return [op.astype(F32).T.astype(BF16) for op in ops]

    def key_operand(block):
        ks = pl.multiple_of(block * tk, tk)
        return jnp.concatenate([k_ref[pl.ds(ks, tk), :], cek_ref[pl.ds(ks, tk), :]], axis=1)

    def value_operands(block):
        ks = pl.multiple_of(block * tk, tk)
        ones = jnp.ones((FOX_SUM_ROWS, tk), BF16)
        return [jnp.concatenate([vt_ref[h * FOX_HEAD_DIM:(h + 1) * FOX_HEAD_DIM, pl.ds(ks, tk)], ones], axis=0)
                for h in range(2)]

    def causal(st, r, tail_block):
        key = lax.broadcasted_iota(jnp.int32, st.shape, 0) + (tail_block * tk - r * qt)
        qry = lax.broadcasted_iota(jnp.int32, st.shape, 1)
        return jnp.where(key <= qry, st, -jnp.inf)

    def write_out(accs):
        for r in range(nt):
            outs = []
            for half in range(2):
                acc = accs[chains.index((half, r))]
                outs.append(acc[:FOX_HEAD_DIM] / acc[FOX_HEAD_DIM:FOX_HEAD_DIM + 1])
            o_ref[r * qt:(r + 1) * qt, :] = jnp.concatenate(outs, axis=0).T.astype(BF16)

    q_ops = query_operands()

    def step(block, slot, carries, attend, score_next, tail_block=None):
        carries = list(carries)
        if attend:
            vts = value_operands(block)
        if score_next:
            k_op = key_operand(block + 1)
        for n in everyone:
            if n in score_next:
                st_ref[1 - slot, n] = jnp.dot(k_op, q_ops[n], preferred_element_type=F32)
            if n in attend:
                half, r = chains[n]
                m, acc = carries[n]
                st = st_ref[slot, n]
                if tail_block is not None:
                    st = causal(st, r, tail_block)
                m_new = jnp.maximum(m, jnp.max(st, axis=0, keepdims=True))
                p = jnp.exp2(st - m_new).astype(BF16)
                acc = jnp.exp2(m - m_new) * acc + jnp.dot(vts[half], p, preferred_element_type=F32)
                carries[n] = (m_new, acc)
        return tuple(carries)

    def body(which, jj, carries):
        for u in range(FOX_LOOP_UNROLL):
            carries = step(FOX_LOOP_UNROLL * jj + u, u % 2, carries, which, which)
        return carries

    carries = tuple((jnp.full((1, qt), -jnp.inf, F32), jnp.zeros((acc_rows, qt), F32)) for _ in chains)
    together = jnp.maximum(first[0], first[1])
    for half in range(2):
        mine = [n for n in everyone if chains[n][0] == half]
        carries = step(first[half] - 1, 1, carries, [], mine)
        carries = lax.fori_loop(first[half] // FOX_LOOP_UNROLL, together // FOX_LOOP_UNROLL,
                                functools.partial(body, mine), carries)
    carries = lax.fori_loop(together // FOX_LOOP_UNROLL, full_blocks // FOX_LOOP_UNROLL,
                            functools.partial(body, everyone), carries)
    for t in range(tail_blocks):
        carries = step(full_blocks + t, t % 2, carries, needs(t), needs(t + 1), tail_block=t)
    write_out([acc for _, acc in carries])


def _fox_first_blocks(logf, gq, gk, tq, tk):
    nq, nk, ratio = logf.shape[0] // tq, logf.shape[0] // tk, tq // tk
    bound = FOX_HEAD_DIM ** 0.5 * jnp.max(jnp.abs(gq)) * jnp.max(jnp.abs(gk)) * LOG2E * 1.02 + 1.0
    block_sums = jnp.sum(logf.reshape(nk, tk, FOX_HEADS), axis=1)
    c_k = jnp.dot(jnp.tril(jnp.ones((nk, nk), F32)), block_sums, precision=lax.Precision.HIGHEST) * LOG2E
    c_q = jnp.concatenate([jnp.zeros((1, FOX_HEADS), F32), c_k[ratio - 1::ratio][:nq - 1]], axis=0)
    key_block = jnp.arange(nk)[None, :, None]
    dead = 2.0 * bound + (c_q[:, None, :] - c_k[None, :, :]) < FOX_UNDERFLOW_LOG2
    dead &= key_block < (jnp.arange(nq) * ratio)[:, None, None]
    first = jnp.min(jnp.where(dead, nk, key_block), axis=1)
    first = first // FOX_LOOP_UNROLL * FOX_LOOP_UNROLL
    return first.T.reshape(-1).astype(jnp.int32)


def _fox_prompt(q16, k16, vt16, ce, logf, gq, gk):
    m = q16.shape[0]
    tq = min(m, FOX_QUERY_BLOCK)
    tk = min(tq // 2, FOX_KEY_BLOCK)
    assert (tq // tk) % FOX_LOOP_UNROLL == 0 and tk % FOX_QUERY_TILE == 0
    pairs = FOX_HEADS // 2
    chains = 2 * (tq // FOX_QUERY_TILE)
    return pl.pallas_call(
        functools.partial(_fox_prompt_kernel, tq=tq, tk=tk),
        grid=(pairs, m // tq),
        in_specs=[
            pl.BlockSpec(memory_space=pltpu.SMEM),
            pl.BlockSpec((tq, LANES), lambda p, i: (i, p)),
            pl.BlockSpec((tq, LANES), lambda p, i: (i, 0)),
            pl.BlockSpec((m, LANES), lambda p, i: (0, p)),
            pl.BlockSpec((m, LANES), lambda p, i: (0, 0)),
            pl.BlockSpec((LANES, m), lambda p, i: (p, 0)),
        ],
        out_specs=pl.BlockSpec((tq, LANES), lambda p, i: (i, p)),
        out_shape=jax.ShapeDtypeStruct((m, FOX_DIM), BF16),
        scratch_shapes=[pltpu.VMEM((2, chains, tk, FOX_QUERY_TILE), F32)],
        compiler_params=_params("parallel", "arbitrary"),
        name="fox_prompt",
    )(_fox_first_blocks(logf, gq, gk, tq, tk), q16, ce, k16, ce, vt16)


def _fox_sample_kernel(q_ref, kn_ref, vn_ref, cen_ref, kc_ref, vc_ref, cec_ref, o_ref, k_all, v_all):
    pair = pl.program_id(1)
    past = kc_ref.shape[0]
    seq = q_ref.shape[0]
    total = k_all.shape[0]
    k_all[0:past, 0:LANES] = kc_ref[...].astype(BF16)
    k_all[0:past, LANES:2 * LANES] = cec_ref[0]
    k_all[past:past + seq, 0:LANES] = kn_ref[...]
    k_all[past:past + seq, LANES:2 * LANES] = cen_ref[...]
    k_all[past + seq:total, :] = jnp.zeros((total - past - seq, 2 * LANES), BF16)
    v_all[0:past, :] = vc_ref[...].astype(BF16)
    v_all[past:past + seq, :] = vn_ref[...]
    v_all[past + seq:total, :] = jnp.zeros((total - past - seq, LANES), BF16)

    q_op = jnp.concatenate([_fox_query_operand(q_ref[...], cen_ref[...], 2 * pair + half, half)
                            for half in range(2)], axis=0)
    s = lax.dot_general(q_op, k_all[...], NT_DIMS, preferred_element_type=F32)
    qr = lax.broadcasted_iota(jnp.int32, s.shape, 0)
    qr = jnp.where(qr >= seq, qr - seq, qr)
    kc = lax.broadcasted_iota(jnp.int32, s.shape, 1)
    s = jnp.where(kc <= qr + past, s, -jnp.inf)
    p = jnp.exp2(s - jnp.max(s, axis=1, keepdims=True))
    l = jnp.sum(p, axis=1, keepdims=True)
    o = jnp.dot(p.astype(BF16), v_all[...], preferred_element_type=F32) / l
    lane = lax.broadcasted_iota(jnp.int32, (seq, LANES), 1)
    o_ref[...] = jnp.where(lane < FOX_HEAD_DIM, o[:seq], o[seq:]).astype(BF16)


def _fox_sample(q16, k16, v16, ce_new, cache_k, cache_v, layer, ce_cache, batch, seq):
    past = cache_k.shape[2]
    total = -(-(past + seq) // LANES) * LANES
    pairs = FOX_HEADS // 2
    new = pl.BlockSpec((seq, LANES), lambda b, p: (b, p))
    cache = pl.BlockSpec((None, None, past, LANES), lambda b, p: (layer, b, 0, p))
    return pl.pallas_call(
        _fox_sample_kernel,
        grid=(batch, pairs),
        in_specs=[new, new, new, pl.BlockSpec((seq, LANES), lambda b, p: (b, 0)),
                  cache, cache, pl.BlockSpec((1, past, LANES), lambda b, p: (b, 0, 0))],
        out_specs=new,
        out_shape=jax.ShapeDtypeStruct((batch * seq, FOX_DIM), BF16),
        scratch_shapes=[pltpu.VMEM((total, 2 * LANES), BF16), pltpu.VMEM((total, LANES), BF16)],
        compiler_params=_params("parallel", "arbitrary"),
        name="fox_sample",
    )(q16, k16, v16, ce_new, cache_k, cache_v, ce_cache)


def _merge_kernel(oa_ref, wa_ref, ob_ref, wb_ref, ga_ref, gb_ref, o_ref):
    a = jnp.dot(oa_ref[...], wa_ref[...], preferred_element_type=F32)
    b = jnp.dot(ob_ref[...], wb_ref[...], preferred_element_type=F32)
    ga = jax.nn.sigmoid(ga_ref[...].astype(F32))
    gb = jax.nn.sigmoid(gb_ref[...].astype(F32))
    o_ref[...] = (ga * a + gb * b).astype(BF16)


def _merge(o_gla, w_a, o_fox, w_b, gates):
    m = o_gla.shape[0]
    tm, tn = min(m, 1024), 1024
    nb = D_MODEL // tn
    act = pl.BlockSpec((tm, D_MODEL), lambda i, j: (i, 0))
    wgt = pl.BlockSpec((D_MODEL, tn), lambda i, j: (0, j))
    return pl.pallas_call(
        _merge_kernel,
        grid=(m // tm, nb),
        in_specs=[act, wgt, act, wgt,
                  pl.BlockSpec((tm, tn), lambda i, j: (i, j)),
                  pl.BlockSpec((tm, tn), lambda i, j: (i, j + nb))],
        out_specs=pl.BlockSpec((tm, tn), lambda i, j: (i, j)),
        out_shape=jax.ShapeDtypeStruct((m, D_MODEL), BF16),
        compiler_params=_params("parallel", "arbitrary"),
        name="merge",
    )(o_gla, w_a, o_fox, w_b, gates, gates)


def _out_proj_kernel(a_ref, w_ref, x_ref, o_ref):
    o_ref[...] = x_ref[...] + jnp.dot(a_ref[...], w_ref[...], preferred_element_type=F32)


def _out_proj(merged, w, x):
    m = x.shape[0]
    tm, tn = min(m, 1024), 1024
    return pl.pallas_call(
        _out_proj_kernel,
        grid=(m // tm, D_MODEL // tn),
        in_specs=[pl.BlockSpec((tm, D_MODEL), lambda i, j: (i, 0)),
                  pl.BlockSpec((D_MODEL, tn), lambda i, j: (0, j)),
                  pl.BlockSpec((tm, tn), lambda i, j: (i, j))],
        out_specs=pl.BlockSpec((tm, tn), lambda i, j: (i, j)),
        out_shape=jax.ShapeDtypeStruct((m, D_MODEL), F32),
        compiler_params=_params("parallel", "arbitrary"),
        name="out_proj",
    )(merged, w, x)


def _mlp_kernel(x_ref, g_ref, wu_ref, wd_ref, o_ref, h_ref):
    @pl.when(pl.program_id(1) == 0)
    def _():
        _rmsnorm_to_bf16(x_ref, g_ref, h_ref)
        o_ref[...] = x_ref[...]

    u = jnp.maximum(jnp.dot(h_ref[...], wu_ref[...], preferred_element_type=F32), 0.0)
    o_ref[...] += jnp.dot((u * u).astype(BF16), wd_ref[...], preferred_element_type=F32)


def _mlp(x, g, w_up, w_down):
    m = x.shape[0]
    tm, tf = min(m, 1024), 512
    return pl.pallas_call(
        _mlp_kernel,
        grid=(m // tm, D_FF // tf),
        in_specs=[pl.BlockSpec((tm, D_MODEL), lambda i, f: (i, 0)),
                  pl.BlockSpec((1, D_MODEL), lambda i, f: (0, 0)),
                  pl.BlockSpec((D_MODEL, tf), lambda i, f: (0, f)),
                  pl.BlockSpec((tf, D_MODEL), lambda i, f: (f, 0))],
        out_specs=pl.BlockSpec((tm, D_MODEL), lambda i, f: (i, 0)),
        out_shape=jax.ShapeDtypeStruct((m, D_MODEL), F32),
        scratch_shapes=[pltpu.VMEM((tm, D_MODEL), BF16)],
        compiler_params=_params("parallel", "arbitrary"),
        name="mlp",
    )(x, g, w_up, w_down)


def _prepare_layer(l, norm_mix_g, w_in, gla_alpha_up, gla_alpha_b, gla_norm_g, fox_f_b, fox_qnorm_g,
                   fox_knorm_g, w_br_gla, w_br_fox, w_out, norm_mlp_g, w_up, w_down):
    w = w_in[l]
    o = 0
    cols = {}
    for name, size in (("gla", 2 * GLA_DK + 2 * GLA_DV), ("low", GLA_LOWRANK), ("fq", FOX_DIM),
                       ("fk", FOX_DIM), ("fv", FOX_DIM), ("ff", FOX_HEADS), ("gate", 2 * D_MODEL)):
        cols[name] = w[:, o:o + size]
        o += size
    pad = LANES - CE_ONES - GLA_LOWRANK
    w_small = jnp.concatenate([cols["ff"]] * CE_PARTS + [cols["low"], jnp.zeros((D_MODEL, pad), F32)], axis=1)
    b_small = jnp.concatenate([fox_f_b[l]] * CE_PARTS + [jnp.zeros((LANES - CE_ONES,), F32)])[None, :]
    alpha_pad = jnp.zeros((LANES, GLA_DK), F32).at[LOWRANK_LANE:LOWRANK_LANE + GLA_LOWRANK].set(gla_alpha_up[l])
    return dict(
        norm_mix_g=norm_mix_g[l][None, :], norm_mlp_g=norm_mlp_g[l][None, :],
        w_gla=cols["gla"].astype(BF16), w_small=w_small.astype(BF16), b_small=b_small,
        w_fq=cols["fq"].astype(BF16), w_fk=cols["fk"].astype(BF16), w_fv=cols["fv"].astype(BF16),
        w_gate=cols["gate"].astype(BF16),
        alpha_pad=alpha_pad.astype(BF16), alpha_b=gla_alpha_b[l][None, :], gla_norm_g=gla_norm_g[l][None, :],
        gq=jnp.tile(fox_qnorm_g[l], FOX_HEADS)[None, :], gk=jnp.tile(fox_knorm_g[l], FOX_HEADS)[None, :],        w_br_gla=w_br_gla[l].astype(BF16), w_br_fox=w_br_fox[l].astype(BF16), w_out=w_out[l].astype(BF16),
        w_up=w_up[l].astype(BF16), w_down=w_down[l].astype(BF16),
    )


def _group_sum_matrix():
    r = lax.broadcasted_iota(jnp.int32, (MXU_DIM, MXU_DIM), 0) // FOX_HEAD_DIM
    c = lax.broadcasted_iota(jnp.int32, (MXU_DIM, MXU_DIM), 1) // FOX_HEAD_DIM
    return (r == c).astype(BF16)


def _layer(x, batch, seq, cache, p, gsum):
    g = p["norm_mix_g"]
    pg = _proj(x, g, p["w_gla"], "plain")
    small = _proj(x, g, p["w_small"], "small", (p["b_small"],))
    q16 = _proj(x, g, p["w_fq"], "qnorm", (gsum, p["gq"]))
    k32, k16 = _proj(x, g, p["w_fk"], "knorm", (gsum, p["gk"]))
    v32, v16 = _proj(x, g, p["w_fv"], "dual_t" if cache is None else "dual")
    gates = _proj(x, g, p["w_gate"], "plain")

    if cache is None:
        assert batch == 1
        o_gla, s_new = _gla(pg, small, p["alpha_pad"], p["alpha_b"], p["gla_norm_g"], None, batch, seq)
        ce = _ce_prompt(small)
        o_fox = _fox_prompt(q16, k16, v16, ce, small[:, :FOX_HEADS], p["gq"], p["gk"])
    else:
        cache_k, cache_v, cache_logf, s0, layer = cache
        depth, past = cache_k.shape[0], cache_k.shape[2]
        o_gla, s_new = _gla(pg, small, p["alpha_pad"], p["alpha_b"], p["gla_norm_g"], s0, batch, seq)
        rep = jnp.concatenate([cache_logf] * CE_PARTS + [jnp.zeros((batch, past, LANES - CE_ONES), F32)], axis=-1)
        ce_cache, ce_new = _ce_sample(rep, small, batch, seq)
        o_fox = _fox_sample(q16, k16, v16, ce_new, cache_k.reshape(depth, batch, past, FOX_DIM),
                            cache_v.reshape(depth, batch, past, FOX_DIM), layer, ce_cache, batch, seq)

    merged = _merge(o_gla, p["w_br_gla"], o_fox, p["w_br_fox"], gates)
    x = _out_proj(merged, p["w_out"], x)
    x = _mlp(x, p["norm_mlp_g"], p["w_up"], p["w_down"])
    k_out = k32.reshape(batch, seq, FOX_HEADS, FOX_HEAD_DIM)
    v_out = v32.reshape(batch, seq, FOX_HEADS, FOX_HEAD_DIM)
    logf = small[:, :FOX_HEADS].reshape(batch, seq, FOX_HEADS)
    return x, k_out, v_out, logf, s_new


def kernel(x_prompt, x_sample, cache_fox_k, cache_fox_v, cache_fox_logf, state_gla, norm_mix_g, w_in,
           gla_alpha_up, gla_alpha_b, gla_norm_g, fox_f_b, fox_qnorm_g, fox_knorm_g, w_br_gla, w_br_fox,
           w_out, norm_mlp_g, w_up, w_down):
    bp, lp, _ = x_prompt.shape
    bs, ls, _ = x_sample.shape
    xp = x_prompt.reshape(bp * lp, D_MODEL)
    xs = x_sample.reshape(bs * ls, D_MODEL)
    gsum = _group_sum_matrix()
    outs_p, outs_s = [], []
    for l in range(w_in.shape[0]):
        p = _prepare_layer(l, norm_mix_g, w_in, gla_alpha_up, gla_alpha_b, gla_norm_g, fox_f_b, fox_qnorm_g,
                           fox_knorm_g, w_br_gla, w_br_fox, w_out, norm_mlp_g, w_up, w_down)
        xp, *rest_p = _layer(xp, bp, lp, None, p, gsum)
        xs, *rest_s = _layer(xs, bs, ls, (cache_fox_k, cache_fox_v, cache_fox_logf[l], state_gla[l], l), p, gsum)
        outs_p.append(rest_p)
        outs_s.append(rest_s)
    stack = lambda outs, i: jnp.stack([o[i] for o in outs])
    return (xp.reshape(bp, lp, D_MODEL), xs.reshape(bs, ls, D_MODEL),
            stack(outs_p, 0), stack(outs_p, 1), stack(outs_p, 2), stack(outs_p, 3),
            stack(outs_s, 0), stack(outs_s, 1), stack(outs_s, 2), stack(outs_s, 3))
```

```python
import functools

import jax
import jax.numpy as jnp
from jax import lax
from jax.experimental import pallas as pl
from jax.experimental.pallas import tpu as pltpu

F32 = jnp.float32
BF16 = jnp.bfloat16

D_MODEL = 2048
DEPTH = 4
GLA_HEADS = 4
GLA_HEAD_K = 256
GLA_HEAD_V = 512
GLA_DK = GLA_HEADS * GLA_HEAD_K
GLA_DV = GLA_HEADS * GLA_HEAD_V
GLA_LOWRANK = 16
GLA_GATE_NORMALIZER = 16.0
GLA_CHUNK = 64
GLA_SUB = 16
FOX_HEADS = 32
FOX_HEAD_DIM = 64
FOX_DIM = FOX_HEADS * FOX_HEAD_DIM
D_FF = 4 * D_MODEL
EPS = 1e-6

LANES = 128
MXU_DIM = 256
VMEM_LIMIT_BYTES = 56 * 1024 * 1024
PROJ_SLAB = 512

CE_PARTS = 3
CE_ONES = CE_PARTS * FOX_HEADS
LOWRANK_LANE = CE_ONES

FOX_QUERY_TILE = MXU_DIM
FOX_QUERY_BLOCK = 1024
FOX_KEY_BLOCK = 512
FOX_SAMPLE_PAIRS = 4
FOX_SUM_ROWS = 16
LOG2E = 1.4426950408889634
FOX_UNDERFLOW_LOG2 = -160.0
FOX_LOOP_UNROLL = 2

NT_DIMS = (((1,), (1,)), ((), ()))
TN_DIMS = (((0,), (0,)), ((), ()))


def _params(*semantics):
    return pltpu.CompilerParams(dimension_semantics=semantics, vmem_limit_bytes=VMEM_LIMIT_BYTES)


def _log_sigmoid(x):
    return jnp.minimum(x, 0.0) - jnp.log(1.0 + jnp.exp(-jnp.abs(x)))


def _split3(x):
    p1 = x.astype(BF16)
    r1 = x - p1.astype(F32)
    p2 = r1.astype(BF16)
    p3 = (r1 - p2.astype(F32)).astype(BF16)
    return p1, p2, p3


def _lower_tri(n):
    r = lax.broadcasted_iota(jnp.int32, (n, n), 0)
    c = lax.broadcasted_iota(jnp.int32, (n, n), 1)
    return jnp.where(r >= c, 1.0, 0.0).astype(BF16)


def _cumsum_rows(tri, x):
    p1, p2, p3 = _split3(x)
    return (jnp.dot(tri, p1, preferred_element_type=F32)
            + jnp.dot(tri, p2, preferred_element_type=F32)
            + jnp.dot(tri, p3, preferred_element_type=F32))


def _rmsnorm_to_bf16(x_ref, g_ref, h_ref, rows=128):
    def body(r, carry):
        sl = pl.ds(pl.multiple_of(r * rows, rows), rows)
        x = x_ref[sl, :]
        ms = jnp.mean(x * x, axis=-1, keepdims=True)
        h_ref[sl, :] = (x * lax.rsqrt(ms + EPS) * g_ref[...]).astype(BF16)
        return carry
    lax.fori_loop(0, x_ref.shape[0] // rows, body, 0)


def _head_rms(acc, gsum_ref, gain):
    sq = acc * acc
    hi = sq.astype(BF16)
    lo = (sq - hi.astype(F32)).astype(BF16)
    parts = []
    for s in range(acc.shape[1] // MXU_DIM):
        sl = slice(s * MXU_DIM, (s + 1) * MXU_DIM)
        parts.append(jnp.dot(hi[:, sl], gsum_ref[...], preferred_element_type=F32)
                     + jnp.dot(lo[:, sl], gsum_ref[...], preferred_element_type=F32))
    ms = jnp.concatenate(parts, axis=1) * (1.0 / FOX_HEAD_DIM)
    return acc * lax.rsqrt(ms + EPS) * gain


def _proj_kernel(x_ref, g_ref, w_ref, *refs, mode):
    h_ref = refs[-1]

    @pl.when(pl.program_id(1) == 0)
    def _():
        _rmsnorm_to_bf16(x_ref, g_ref, h_ref)

    h = h_ref[...]
    slab = min(w_ref.shape[1], PROJ_SLAB)
    for s in range(w_ref.shape[1] // slab):
        cols = slice(s * slab, (s + 1) * slab)
        acc = jnp.dot(h, w_ref[:, cols], preferred_element_type=F32)
        if mode == "plain":
            (o_ref,) = refs[:-1]
            o_ref[:, cols] = acc.astype(o_ref.dtype)
        elif mode == "small":
            b_ref, o_ref = refs[:-1]
            lane = lax.broadcasted_iota(jnp.int32, acc.shape, 1)
            o_ref[:, cols] = jnp.where(lane < CE_ONES, _log_sigmoid(acc + b_ref[:, cols]), acc)
        elif mode == "qnorm":
            gsum_ref, gain_ref, o_ref = refs[:-1]
            y = _head_rms(acc, gsum_ref, gain_ref[:, cols])
            o_ref[:, cols] = (y * (FOX_HEAD_DIM ** -0.5 * LOG2E)).astype(BF16)
        elif mode == "knorm":
            gsum_ref, gain_ref, o32_ref, o16_ref = refs[:-1]
            y = _head_rms(acc, gsum_ref, gain_ref[:, cols])
            o32_ref[:, cols] = y
            o16_ref[:, cols] = y.astype(BF16)
        elif mode == "dual":
            o32_ref, o16_ref = refs[:-1]
            o32_ref[:, cols] = acc
            o16_ref[:, cols] = acc.astype(BF16)
        else:
            o32_ref, o16t_ref = refs[:-1]
            o32_ref[:, cols] = acc
            o16t_ref[cols, :] = acc.T.astype(BF16)


def _proj(x, g, w, mode, extras=()):
    m, n = x.shape[0], w.shape[1]
    tm = min(m, 1024)
    tn = min(n, 2 * PROJ_SLAB)
    grid = (m // tm, n // tn)
    in_specs = [
        pl.BlockSpec((tm, D_MODEL), lambda i, j: (i, 0)),
        pl.BlockSpec((1, D_MODEL), lambda i, j: (0, 0)),
        pl.BlockSpec((D_MODEL, tn), lambda i, j: (0, j)),
    ]
    tile = pl.BlockSpec((tm, tn), lambda i, j: (i, j))
    row = pl.BlockSpec((1, tn), lambda i, j: (0, j))
    if mode == "plain":
        out_shape, out_specs = jax.ShapeDtypeStruct((m, n), BF16), tile
    elif mode == "small":
        in_specs += [row]
        out_shape, out_specs = jax.ShapeDtypeStruct((m, n), F32), tile
    elif mode == "qnorm":
        in_specs += [pl.BlockSpec((MXU_DIM, MXU_DIM), lambda i, j: (0, 0)), row]
        out_shape, out_specs = jax.ShapeDtypeStruct((m, n), BF16), tile
    elif mode == "dual_t":
        out_shape = (jax.ShapeDtypeStruct((m, n), F32), jax.ShapeDtypeStruct((n, m), BF16))
        out_specs = (tile, pl.BlockSpec((tn, tm), lambda i, j: (j, i)))
    else:
        if mode == "knorm":
            in_specs += [pl.BlockSpec((MXU_DIM, MXU_DIM), lambda i, j: (0, 0)), row]
        out_shape = (jax.ShapeDtypeStruct((m, n), F32), jax.ShapeDtypeStruct((m, n), BF16))
        out_specs = (tile, tile)
    return pl.pallas_call(
        functools.partial(_proj_kernel, mode=mode),
        grid=grid, in_specs=in_specs, out_specs=out_specs, out_shape=out_shape,
        scratch_shapes=[pltpu.VMEM((tm, D_MODEL), BF16)],
        compiler_params=_params("parallel", "arbitrary"),
        name="proj_" + mode,
    )(x, g, w, *extras)


def _encode_cumsum(cs):
    c1, c2, c3 = _split3(cs * LOG2E)
    lane = lax.broadcasted_iota(jnp.int32, cs.shape, 1)
    one = jnp.ones_like(c1)
    return jnp.where(lane < FOX_HEADS, c1,
                     jnp.where(lane < 2 * FOX_HEADS, c2, jnp.where(lane < CE_ONES, c3, one)))


def _ce_prompt_kernel(s_ref, ce_ref, carry_ref):
    @pl.when(pl.program_id(0) == 0)
    def _():
        carry_ref[...] = jnp.zeros_like(carry_ref)

    t = s_ref.shape[0]
    cs = _cumsum_rows(_lower_tri(t), s_ref[...]) + carry_ref[...]
    carry_ref[...] = cs[t - 1:t, :]
    ce_ref[...] = _encode_cumsum(cs)


def _ce_prompt(small):
    m = small.shape[0]
    t = MXU_DIM
    return pl.pallas_call(
        _ce_prompt_kernel,
        grid=(m // t,),
        in_specs=[pl.BlockSpec((t, LANES), lambda i: (i, 0))],
        out_specs=pl.BlockSpec((t, LANES), lambda i: (i, 0)),
        out_shape=jax.ShapeDtypeStruct((m, LANES), BF16),
        scratch_shapes=[pltpu.VMEM((1, LANES), F32)],
        compiler_params=_params("arbitrary"),
        name="ce_prompt",
    )(small)


def _ce_sample_kernel(c_ref, s_ref, cec_ref, cen_ref):
    t = MXU_DIM
    tri = _lower_tri(t)
    carry = jnp.zeros((1, LANES), F32)
    for b in range(c_ref.shape[1] // t):
        cs = _cumsum_rows(tri, c_ref[0, b * t:(b + 1) * t, :]) + carry
        carry = cs[t - 1:t, :]
        cec_ref[0, b * t:(b + 1) * t, :] = _encode_cumsum(cs)
    n = s_ref.shape[0]
    cs = _cumsum_rows(_lower_tri(n), s_ref[...]) + carry
    cen_ref[...] = _encode_cumsum(cs)


def _ce_sample(cache_rep, small, batch, seq):
    past = cache_rep.shape[1]
    return pl.pallas_call(
        _ce_sample_kernel,
        grid=(batch,),
        in_specs=[pl.BlockSpec((1, past, LANES), lambda b: (b, 0, 0)),
                  pl.BlockSpec((seq, LANES), lambda b: (b, 0))],
        out_specs=(pl.BlockSpec((1, past, LANES), lambda b: (b, 0, 0)),
                   pl.BlockSpec((seq, LANES), lambda b: (b, 0))),
        out_shape=(jax.ShapeDtypeStruct((batch, past, LANES), BF16),
                   jax.ShapeDtypeStruct((batch * seq, LANES), BF16)),
        compiler_params=_params("parallel"),
        name="ce_sample",
    )(cache_rep, small)


def _gla_kernel(*refs, chunk, has_state):
    if has_state:
        (q_ref, k_ref, v_ref, gg_ref, sm_ref, au_ref, ab_ref, gn_ref, s0_ref,
         o_ref, sout_ref, st_ref) = refs
    else:
        (q_ref, k_ref, v_ref, gg_ref, sm_ref, au_ref, ab_ref, gn_ref,
         o_ref, sout_ref, st_ref) = refs
    blk = pl.program_id(1)
    hk, hv = GLA_HEAD_K, GLA_HEAD_V
    tb = q_ref.shape[0]
    n_chunks, n_sub = tb // chunk, chunk // GLA_SUB
    heads = range(GLA_HEADS)

    @pl.when(blk == 0)
    def _():
        for h in heads:
            st_ref[h] = s0_ref[0, h].T if has_state else jnp.zeros((hv, hk), F32)

    a = jnp.dot(sm_ref[...].astype(BF16), au_ref[...], preferred_element_type=F32) + ab_ref[...]
    log_a = _log_sigmoid(a) * (1.0 / GLA_GATE_NORMALIZER)
    r = lax.broadcasted_iota(jnp.int32, (tb, tb), 0)
    c = lax.broadcasted_iota(jnp.int32, (tb, tb), 1)
    tri = jnp.where((r >= c) & (r // chunk == c // chunk), 1.0, 0.0).astype(BF16)
    bc = _cumsum_rows(tri, log_a)

    def rows_like(row_of, group):
        return jnp.concatenate([jnp.broadcast_to(bc[row_of(g):row_of(g) + 1, :], (group, bc.shape[1]))
                                for g in range(tb // group)], axis=0)

    last = rows_like(lambda g: (g + 1) * chunk - 1, chunk)
    anchor = rows_like(lambda g: g * GLA_SUB, GLA_SUB)
    scale = hk ** -0.5
    q = q_ref[...].astype(F32)
    k = k_ref[...].astype(F32)
    q_inter = (q * jnp.exp(bc) * scale).astype(BF16)
    k_dec = (k * jnp.exp(last - bc)).astype(BF16)
    q_sub = (q * jnp.exp(bc - anchor) * scale).astype(BF16)

    att = {}
    for ch in range(n_chunks):
        r0 = ch * chunk
        q_slots, k_slots = [], []
        for i in range(n_sub):
            lo, hi = i * GLA_SUB, (i + 1) * GLA_SUB
            k_i = (k[r0:r0 + hi] * jnp.exp(anchor[r0 + lo:r0 + lo + 1] - bc[r0:r0 + hi])).astype(BF16)
            k_slots.append(jnp.concatenate([k_i, jnp.zeros((chunk - hi, k.shape[1]), BF16)], axis=0)
                           if hi < chunk else k_i)
            pieces = [jnp.zeros((lo, q.shape[1]), BF16)] if lo else []
            pieces.append(q_sub[r0 + lo:r0 + hi])
            if hi < chunk:
                pieces.append(jnp.zeros((chunk - hi, q.shape[1]), BF16))
            q_slots.append(jnp.concatenate(pieces, axis=0) if len(pieces) > 1 else pieces[0])
        for h in heads:
            kc = slice(h * hk, (h + 1) * hk)
            q_hat = jnp.concatenate([s[:, kc] for s in q_slots], axis=1)
            k_hat = jnp.concatenate([s[:, kc] for s in k_slots], axis=1)
            att[ch, h] = lax.dot_general(q_hat, k_hat, NT_DIMS, preferred_element_type=F32)
    ri = lax.broadcasted_iota(jnp.int32, (chunk, chunk), 0)
    ci = lax.broadcasted_iota(jnp.int32, (chunk, chunk), 1)
    o_intra = {}
    for (ch, h), s in att.items():
        p = jnp.where(ci <= ri, s, 0.0).astype(BF16)
        o_intra[ch, h] = jnp.dot(p, v_ref[ch * chunk:(ch + 1) * chunk, h * hv:(h + 1) * hv],
                                 preferred_element_type=F32)

    for ch in range(n_chunks):
        rows = slice(ch * chunk, (ch + 1) * chunk)
        decay = jnp.exp(bc[(ch + 1) * chunk - 1:(ch + 1) * chunk, :])
        for h in heads:
            kc = slice(h * hk, (h + 1) * hk)
            vc = slice(h * hv, (h + 1) * hv)
            st = st_ref[h]
            o = o_intra[ch, h] + lax.dot_general(q_inter[rows, kc], st.astype(BF16), NT_DIMS,
                                                 preferred_element_type=F32)
            st_ref[h] = st * decay[:, kc] + lax.dot_general(v_ref[rows, vc], k_dec[rows, kc], TN_DIMS,
                                                            preferred_element_type=F32)
            ms = jnp.mean(o * o, axis=-1, keepdims=True)
            y = o * lax.rsqrt(ms + EPS) * gn_ref[...]
            gg = gg_ref[rows, vc].astype(F32)
            o_ref[rows, vc] = (y * (gg * jax.nn.sigmoid(gg))).astype(BF16)

    @pl.when(blk == pl.num_programs(1) - 1)
    def _():
        for h in heads:
            sout_ref[0, h] = st_ref[h].T


def _gla(pg, small, alpha_pad, alpha_b, gnorm, s0, batch, seq):
    chunk = min(GLA_CHUNK, seq)
    tb = min(seq, 4 * chunk)
    nblk = seq // tb
    hk, hv = GLA_HEAD_K, GLA_HEAD_V
    row = lambda b, t: b * nblk + t
    in_specs = [
        pl.BlockSpec((tb, GLA_DK), lambda b, t: (row(b, t), 0)),
        pl.BlockSpec((tb, GLA_DK), lambda b, t: (row(b, t), 1)),
        pl.BlockSpec((tb, GLA_DV), lambda b, t: (row(b, t), 1)),
        pl.BlockSpec((tb, GLA_DV), lambda b, t: (row(b, t), 2)),
        pl.BlockSpec((tb, LANES), lambda b, t: (row(b, t), 0)),
        pl.BlockSpec((LANES, GLA_DK), lambda b, t: (0, 0)),
        pl.BlockSpec((1, GLA_DK), lambda b, t: (0, 0)),
        pl.BlockSpec((1, hv), lambda b, t: (0, 0)),
    ]
    args = [pg, pg, pg, pg, small, alpha_pad, alpha_b, gnorm]
    state_spec = pl.BlockSpec((1, GLA_HEADS, hk, hv), lambda b, t: (b, 0, 0, 0))
    if s0 is not None:
        in_specs.append(state_spec)
        args.append(s0)
    return pl.pallas_call(
        functools.partial(_gla_kernel, chunk=chunk, has_state=s0 is not None),
        grid=(batch, nblk),
        in_specs=in_specs,
        out_specs=(pl.BlockSpec((tb, GLA_DV), lambda b, t: (row(b, t), 0)), state_spec),
        out_shape=(jax.ShapeDtypeStruct((batch * seq, GLA_DV), BF16),
                   jax.ShapeDtypeStruct((batch, GLA_HEADS, hk, hv), F32)),
        scratch_shapes=[pltpu.VMEM((GLA_HEADS, hv, hk), F32)],
        compiler_params=_params("parallel", "arbitrary"),
        name="gla",
    )(*args)


def _fox_query_operand(q, ce_q, head, half):
    lane = lax.broadcasted_iota(jnp.int32, (1, LANES), 1)
    keep = jnp.where((lane >= half * FOX_HEAD_DIM) & (lane < (half + 1) * FOX_HEAD_DIM), 1.0, 0.0)
    q_m = q * keep.astype(BF16)
    r = lax.broadcasted_iota(jnp.int32, (LANES, LANES), 0)
    c = lax.broadcasted_iota(jnp.int32, (LANES, LANES), 1)
    pick = jnp.zeros((LANES, LANES), F32)
    for part in range(CE_PARTS):
        pick = jnp.where((r == head + part * FOX_HEADS) & (c == CE_ONES + part), 1.0, pick)
    enc = jnp.dot(ce_q, pick.astype(BF16), preferred_element_type=F32)
    minus = jnp.zeros((1, LANES), F32)
    for part in range(CE_PARTS):
        minus = jnp.where(lane == head + part * FOX_HEADS, -1.0, minus)
    return jnp.concatenate([q_m, (enc + minus).astype(BF16)], axis=1)


def _fox_prompt_kernel(first_ref, q_ref, ceq_ref, k_ref, cek_ref, vt_ref, o_ref, st_ref, *, tq, tk):
    pair = pl.program_id(0)
    qi = pl.program_id(1)
    first = [first_ref[(2 * pair + half) * pl.num_programs(1) + qi] for half in range(2)]
    qt = FOX_QUERY_TILE
    nt = tq // qt
    tail_blocks = tq // tk
    full_blocks = qi * tail_blocks
    chains = [(half, r) for half in range(2) for r in range(nt)]
    everyone = list(range(len(chains)))
    needs = lambda t: [n for n in everyone if t < tail_blocks and (chains[n][1] + 1) * qt > t * tk]
    acc_rows = FOX_HEAD_DIM + FOX_SUM_ROWS

    def query_operands():
        ops = [_fox_query_operand(q_ref[r * qt:(r + 1) * qt, :], ceq_ref[r * qt:(r + 1) * qt, :],
                                  2 * pair + half, half) for half, r in chains]
        return [op.astype(F32).T.astype(BF16) for op in ops]

    def key_operand(block):
        ks = pl.multiple_of(block * tk, tk)
        return jnp.concatenate([k_ref[pl.ds(ks, tk), :], cek_ref[pl.ds(ks, tk), :]], axis=1)

    def value_operands(block):
        ks = pl.multiple_of(block * tk, tk)
        ones = jnp.ones((FOX_SUM_ROWS, tk), BF16)
        return [jnp.concatenate([vt_ref[h * FOX_HEAD_DIM:(h + 1) * FOX_HEAD_DIM, pl.ds(ks, tk)], ones], axis=0)
                for h in range(2)]

    def causal(st, r, tail_block):
        key = lax.broadcasted_iota(jnp.int32, st.shape, 0) + (tail_block * tk - r * qt)
        qry = lax.broadcasted_iota(jnp.int32, st.shape, 1)
        return jnp.where(key <= qry, st, -jnp.inf)

    def write_out(accs):
        for r in range(nt):
            outs = []
            for half in range(2):
                acc = accs[chains.index((half, r))]
                outs.append(acc[:FOX_HEAD_DIM] / acc[FOX_HEAD_DIM:FOX_HEAD_DIM + 1])
            o_ref[r * qt:(r + 1) * qt, :] = jnp.concatenate(outs, axis=0).T.astype(BF16)

    q_ops = query_operands()

    def step(block, slot, carries, attend, score_next, tail_block=None):
        carries = list(carries)
        if attend:
            vts = value_operands(block)
        if score_next:
            k_op = key_operand(block + 1)
        for n in everyone:
            if n in score_next:
                st_ref[1 - slot, n] = jnp.dot(k_op, q_ops[n], preferred_element_type=F32)
            if n in attend:
                half, r = chains[n]
                m, acc = carries[n]
                st = st_ref[slot, n]
                if tail_block is not None:
                    st = causal(st, r, tail_block)
                m_new = jnp.maximum(m, jnp.max(st, axis=0, keepdims=True))
                p = jnp.exp2(st - m_new).astype(BF16)
                acc = jnp.exp2(m - m_new) * acc + jnp.dot(vts[half], p, preferred_element_type=F32)
                carries[n] = (m_new, acc)
        return tuple(carries)

    def body(which, jj, carries):
        for u in range(FOX_LOOP_UNROLL):
            carries = step(FOX_LOOP_UNROLL * jj + u, u % 2, carries, which, which)
        return carries

    carries = tuple((jnp.full((1, qt), -jnp.inf, F32), jnp.zeros((acc_rows, qt), F32)) for _ in chains)
    together = jnp.maximum(first[0], first[1])
    for half in range(2):
        mine = [n for n in everyone if chains[n][0] == half]
        carries = step(first[half] - 1, 1, carries, [], mine)
        carries = lax.fori_loop(first[half] // FOX_LOOP_UNROLL, together // FOX_LOOP_UNROLL,
                                functools.partial(body, mine), carries)
    carries = lax.fori_loop(together // FOX_LOOP_UNROLL, full_blocks // FOX_LOOP_UNROLL,
                            functools.partial(body, everyone), carries)
    for t in range(tail_blocks):
        carries = step(full_blocks + t, t % 2, carries, needs(t), needs(t + 1), tail_block=t)
    write_out([acc for _, acc in carries])


def _fox_first_blocks(logf, gq, gk, tq, tk):
    nq, nk, ratio = logf.shape[0] // tq, logf.shape[0] // tk, tq // tk
    bound = FOX_HEAD_DIM ** 0.5 * jnp.max(jnp.abs(gq)) * jnp.max(jnp.abs(gk)) * LOG2E * 1.02 + 1.0
    block_sums = jnp.sum(logf.reshape(nk, tk, FOX_HEADS), axis=1)
    c_k = jnp.dot(jnp.tril(jnp.ones((nk, nk), F32)), block_sums, precision=lax.Precision.HIGHEST) * LOG2E
    c_q = jnp.concatenate([jnp.zeros((1, FOX_HEADS), F32), c_k[ratio - 1::ratio][:nq - 1]], axis=0)
    key_block = jnp.arange(nk)[None, :, None]
    dead = 2.0 * bound + (c_q[:, None, :] - c_k[None, :, :]) < FOX_UNDERFLOW_LOG2
    dead &= key_block < (jnp.arange(nq) * ratio)[:, None, None]
    first = jnp.min(jnp.where(dead, nk, key_block), axis=1)
    first = first // FOX_LOOP_UNROLL * FOX_LOOP_UNROLL
    return first.T.reshape(-1).astype(jnp.int32)


def _fox_prompt(q16, k16, vt16, ce, logf, gq, gk):
    m = q16.shape[0]
    tq = min(m, FOX_QUERY_BLOCK)
    tk = min(tq // 2, FOX_KEY_BLOCK)
    assert (tq // tk) % FOX_LOOP_UNROLL == 0 and tk % FOX_QUERY_TILE == 0
    pairs = FOX_HEADS // 2
    chains = 2 * (tq // FOX_QUERY_TILE)
    return pl.pallas_call(
        functools.partial(_fox_prompt_kernel, tq=tq, tk=tk),
        grid=(pairs, m // tq),
        in_specs=[
            pl.BlockSpec(memory_space=pltpu.SMEM),
            pl.BlockSpec((tq, LANES), lambda p, i: (i, p)),
            pl.BlockSpec((tq, LANES), lambda p, i: (i, 0)),
            pl.BlockSpec((m, LANES), lambda p, i: (0, p)),
            pl.BlockSpec((m, LANES), lambda p, i: (0, 0)),
            pl.BlockSpec((LANES, m), lambda p, i: (p, 0)),
        ],
        out_specs=pl.BlockSpec((tq, LANES), lambda p, i: (i, p)),
        out_shape=jax.ShapeDtypeStruct((m, FOX_DIM), BF16),
        scratch_shapes=[pltpu.VMEM((2, chains, tk, FOX_QUERY_TILE), F32)],
        compiler_params=_params("parallel", "arbitrary"),
        name="fox_prompt",
    )(_fox_first_blocks(logf, gq, gk, tq, tk), q16, ce, k16, ce, vt16)


def _fox_sample_kernel(q_ref, kn_ref, vn_ref, cen_ref, kc_ref, vc_ref, cec_ref, o_ref, k_all, v_all):
    group = pl.program_id(1)
    past = kc_ref.shape[0]
    seq = q_ref.shape[0]
    total = k_all.shape[1]
    pairs = range(FOX_SAMPLE_PAIRS)
    for u in pairs:
        lanes = slice(u * LANES, (u + 1) * LANES)
        k_all[u, 0:past, 0:LANES] = kc_ref[:, lanes].astype(BF16)
        k_all[u, 0:past, LANES:2 * LANES] = cec_ref[0]
        k_all[u, past:past + seq, 0:LANES] = kn_ref[:, lanes]
        k_all[u, past:past + seq, LANES:2 * LANES] = cen_ref[...]
        k_all[u, past + seq:total, :] = jnp.zeros((total - past - seq, 2 * LANES), BF16)
        v_all[u, 0:past, :] = vc_ref[:, lanes].astype(BF16)
        v_all[u, past:past + seq, :] = vn_ref[:, lanes]
        v_all[u, past + seq:total, :] = jnp.zeros((total - past - seq, LANES), BF16)

    q_ops = [jnp.concatenate([_fox_query_operand(q_ref[:, u * LANES:(u + 1) * LANES], cen_ref[...],
                                                 2 * (FOX_SAMPLE_PAIRS * group + u) + half, half)
                              for half in range(2)], axis=0) for u in pairs]
    scores = [lax.dot_general(q_ops[u], k_all[u], NT_DIMS, preferred_element_type=F32) for u in pairs]
    qr = lax.broadcasted_iota(jnp.int32, scores[0].shape, 0)
    qr = jnp.where(qr >= seq, qr - seq, qr)
    kc = lax.broadcasted_iota(jnp.int32, scores[0].shape, 1)
    probs, sums = [], []
    for s in scores:
        s = jnp.where(kc <= qr + past, s, -jnp.inf)
        p = jnp.exp2(s - jnp.max(s, axis=1, keepdims=True))
        sums.append(jnp.sum(p, axis=1, keepdims=True))
        probs.append(p.astype(BF16))
    lane = lax.broadcasted_iota(jnp.int32, (seq, LANES), 1)
    for u in pairs:
        o = jnp.dot(probs[u], v_all[u], preferred_element_type=F32) / sums[u]
        o_ref[:, u * LANES:(u + 1) * LANES] = jnp.where(lane < FOX_HEAD_DIM, o[:seq], o[seq:]).astype(BF16)


def _fox_sample(q16, k16, v16, ce_new, cache_k, cache_v, layer, ce_cache, batch, seq):
    past = cache_k.shape[2]
    total = -(-(past + seq) // LANES) * LANES
    width = FOX_SAMPLE_PAIRS * LANES
    groups = FOX_DIM // width
    new = pl.BlockSpec((seq, width), lambda b, p: (b, p))
    cache = pl.BlockSpec((None, None, past, width), lambda b, p: (layer, b, 0, p))
    return pl.pallas_call(
        _fox_sample_kernel,
        grid=(batch, groups),
        in_specs=[new, new, new, pl.BlockSpec((seq, LANES), lambda b, p: (b, 0)),
                  cache, cache, pl.BlockSpec((1, past, LANES), lambda b, p: (b, 0, 0))],
        out_specs=new,
        out_shape=jax.ShapeDtypeStruct((batch * seq, FOX_DIM), BF16),
        scratch_shapes=[pltpu.VMEM((FOX_SAMPLE_PAIRS, total, 2 * LANES), BF16),
                        pltpu.VMEM((FOX_SAMPLE_PAIRS, total, LANES), BF16)],
        compiler_params=_params("parallel", "arbitrary"),
        name="fox_sample",
    )(q16, k16, v16, ce_new, cache_k, cache_v, ce_cache)


def _merge_kernel(oa_ref, wa_ref, ob_ref, wb_ref, ga_ref, gb_ref, o_ref):
    a = jnp.dot(oa_ref[...], wa_ref[...], preferred_element_type=F32)
    b = jnp.dot(ob_ref[...], wb_ref[...], preferred_element_type=F32)
    ga = jax.nn.sigmoid(ga_ref[...].astype(F32))
    gb = jax.nn.sigmoid(gb_ref[...].astype(F32))
    o_ref[...] = (ga * a + gb * b).astype(BF16)


def _merge(o_gla, w_a, o_fox, w_b, gates):
    m = o_gla.shape[0]
    tm, tn = min(m, 1024), 1024
    nb = D_MODEL // tn
    act = pl.BlockSpec((tm, D_MODEL), lambda i, j: (i, 0))
    wgt = pl.BlockSpec((D_MODEL, tn), lambda i, j: (0, j))
    return pl.pallas_call(
        _merge_kernel,
        grid=(m // tm, nb),
        in_specs=[act, wgt, act, wgt,
                  pl.BlockSpec((tm, tn), lambda i, j: (i, j)),
                  pl.BlockSpec((tm, tn), lambda i, j: (i, j + nb))],
        out_specs=pl.BlockSpec((tm, tn), lambda i, j: (i, j)),
        out_shape=jax.ShapeDtypeStruct((m, D_MODEL), BF16),
        compiler_params=_params("parallel", "arbitrary"),
        name="merge",
    )(o_gla, w_a, o_fox, w_b, gates, gates)


def _out_proj_kernel(a_ref, w_ref, x_ref, o_ref):
    o_ref[...] = x_ref[...] + jnp.dot(a_ref[...], w_ref[...], preferred_element_type=F32)


def _out_proj(merged, w, x):
    m = x.shape[0]
    tm, tn = min(m, 1024), 1024
    return pl.pallas_call(
        _out_proj_kernel,
        grid=(m // tm, D_MODEL // tn),
        in_specs=[pl.BlockSpec((tm, D_MODEL), lambda i, j: (i, 0)),
                  pl.BlockSpec((D_MODEL, tn), lambda i, j: (0, j)),
                  pl.BlockSpec((tm, tn), lambda i, j: (i, j))],
        out_specs=pl.BlockSpec((tm, tn), lambda i, j: (i, j)),
        out_shape=jax.ShapeDtypeStruct((m, D_MODEL), F32),
        compiler_params=_params("parallel", "arbitrary"),
        name="out_proj",
    )(merged, w, x)


def _mlp_kernel(x_ref, g_ref, wu_ref, wd_ref, o_ref, h_ref):
    @pl.when(pl.program_id(1) == 0)
    def _():
        _rmsnorm_to_bf16(x_ref, g_ref, h_ref)
        o_ref[...] = x_ref[...]

    u = jnp.maximum(jnp.dot(h_ref[...], wu_ref[...], preferred_element_type=F32), 0.0)
    o_ref[...] += jnp.dot((u * u).astype(BF16), wd_ref[...], preferred_element_type=F32)


def _mlp(x, g, w_up, w_down):
    m = x.shape[0]
    tm, tf = min(m, 1024), 512
    return pl.pallas_call(
        _mlp_kernel,
        grid=(m // tm, D_FF // tf),
        in_specs=[pl.BlockSpec((tm, D_MODEL), lambda i, f: (i, 0)),
                  pl.BlockSpec((1, D_MODEL), lambda i, f: (0, 0)),
                  pl.BlockSpec((D_MODEL, tf), lambda i, f: (0, f)),
                  pl.BlockSpec((tf, D_MODEL), lambda i, f: (f, 0))],
        out_specs=pl.BlockSpec((tm, D_MODEL), lambda i, f: (i, 0)),
        out_shape=jax.ShapeDtypeStruct((m, D_MODEL), F32),
        scratch_shapes=[pltpu.VMEM((tm, D_MODEL), BF16)],
        compiler_params=_params("parallel", "arbitrary"),
        name="mlp",
    )(x, g, w_up, w_down)


def _prepare_layer(l, norm_mix_g, w_in, gla_alpha_up, gla_alpha_b, gla_norm_g, fox_f_b, fox_qnorm_g,
                   fox_knorm_g, w_br_gla, w_br_fox, w_out, norm_mlp_g, w_up, w_down):
    w = w_in[l]
    o = 0
    cols = {}
    for name, size in (("gla", 2 * GLA_DK + 2 * GLA_DV), ("low", GLA_LOWRANK), ("fq", FOX_DIM),
                       ("fk", FOX_DIM), ("fv", FOX_DIM), ("ff", FOX_HEADS), ("gate", 2 * D_MODEL)):
        cols[name] = w[:, o:o + size]
        o += size
    pad = LANES - CE_ONES - GLA_LOWRANK
    w_small = jnp.concatenate([cols["ff"]] * CE_PARTS + [cols["low"], jnp.zeros((D_MODEL, pad), F32)], axis=1)
    b_small = jnp.concatenate([fox_f_b[l]] * CE_PARTS + [jnp.zeros((LANES - CE_ONES,), F32)])[None, :]
    alpha_pad = jnp.zeros((LANES, GLA_DK), F32).at[LOWRANK_LANE:LOWRANK_LANE + GLA_LOWRANK].set(gla_alpha_up[l])
    return dict(
        norm_mix_g=norm_mix_g[l][None, :], norm_mlp_g=norm_mlp_g[l][None, :],
        w_gla=cols["gla"].astype(BF16), w_small=w_small.astype(BF16), b_small=b_small,
        w_fq=cols["fq"].astype(BF16), w_fk=cols["fk"].astype(BF16), w_fv=cols["fv"].astype(BF16),
        w_gate=cols["gate"].astype(BF16),
        alpha_pad=alpha_pad.astype(BF16), alpha_b=gla_alpha_b[l][None, :], gla_norm_g=gla_norm_g[l][None, :],
        gq=jnp.tile(fox_qnorm_g[l], FOX_HEADS)[None, :], gk=jnp.tile(fox_knorm_g[l], FOX_HEADS)[None, :],        w_br_gla=w_br_gla[l].astype(BF16), w_br_fox=w_br_fox[l].astype(BF16), w_out=w_out[l].astype(BF16),
        w_up=w_up[l].astype(BF16), w_down=w_down[l].astype(BF16),
    )


def _group_sum_matrix():
    r = lax.broadcasted_iota(jnp.int32, (MXU_DIM, MXU_DIM), 0) // FOX_HEAD_DIM
    c = lax.broadcasted_iota(jnp.int32, (MXU_DIM, MXU_DIM), 1) // FOX_HEAD_DIM
    return (r == c).astype(BF16)


def _layer(x, batch, seq, cache, p, gsum):
    g = p["norm_mix_g"]
    pg = _proj(x, g, p["w_gla"], "plain")
    small = _proj(x, g, p["w_small"], "small", (p["b_small"],))
    q16 = _proj(x, g, p["w_fq"], "qnorm", (gsum, p["gq"]))
    k32, k16 = _proj(x, g, p["w_fk"], "knorm", (gsum, p["gk"]))
    v32, v16 = _proj(x, g, p["w_fv"], "dual_t" if cache is None else "dual")
    gates = _proj(x, g, p["w_gate"], "plain")

    if cache is None:
        assert batch == 1
        o_gla, s_new = _gla(pg, small, p["alpha_pad"], p["alpha_b"], p["gla_norm_g"], None, batch, seq)
        ce = _ce_prompt(small)
        o_fox = _fox_prompt(q16, k16, v16, ce, small[:, :FOX_HEADS], p["gq"], p["gk"])
    else:
        cache_k, cache_v, cache_logf, s0, layer = cache
        depth, past = cache_k.shape[0], cache_k.shape[2]
        o_gla, s_new = _gla(pg, small, p["alpha_pad"], p["alpha_b"], p["gla_norm_g"], s0, batch, seq)
        rep = jnp.concatenate([cache_logf] * CE_PARTS + [jnp.zeros((batch, past, LANES - CE_ONES), F32)], axis=-1)
        ce_cache, ce_new = _ce_sample(rep, small, batch, seq)
        o_fox = _fox_sample(q16, k16, v16, ce_new, cache_k.reshape(depth, batch, past, FOX_DIM),
                            cache_v.reshape(depth, batch, past, FOX_DIM), layer, ce_cache, batch, seq)

    merged = _merge(o_gla, p["w_br_gla"], o_fox, p["w_br_fox"], gates)
    x = _out_proj(merged, p["w_out"], x)
    x = _mlp(x, p["norm_mlp_g"], p["w_up"], p["w_down"])
    k_out = k32.reshape(batch, seq, FOX_HEADS, FOX_HEAD_DIM)
    v_out = v32.reshape(batch, seq, FOX_HEADS, FOX_HEAD_DIM)
    logf = small[:, :FOX_HEADS].reshape(batch, seq, FOX_HEADS)
    return x, k_out, v_out, logf, s_new


def kernel(x_prompt, x_sample, cache_fox_k, cache_fox_v, cache_fox_logf, state_gla, norm_mix_g, w_in,
           gla_alpha_up, gla_alpha_b, gla_norm_g, fox_f_b, fox_qnorm_g, fox_knorm_g, w_br_gla, w_br_fox,
           w_out, norm_mlp_g, w_up, w_down):
    bp, lp, _ = x_prompt.shape
    bs, ls, _ = x_sample.shape
    xp = x_prompt.reshape(bp * lp, D_MODEL)
    xs = x_sample.reshape(bs * ls, D_MODEL)
    gsum = _group_sum_matrix()
    outs_p, outs_s = [], []
    for l in range(w_in.shape[0]):
        p = _prepare_layer(l, norm_mix_g, w_in, gla_alpha_up, gla_alpha_b, gla_norm_g, fox_f_b, fox_qnorm_g,
                           fox_knorm_g, w_br_gla, w_br_fox, w_out, norm_mlp_g, w_up, w_down)
        xp, *rest_p = _layer(xp, bp, lp, None, p, gsum)
        xs, *rest_s = _layer(xs, bs, ls, (cache_fox_k, cache_fox_v, cache_fox_logf[l], state_gla[l], l), p, gsum)
        outs_p.append(rest_p)
        outs_s.append(rest_s)
    stack = lambda outs, i: jnp.stack([o[i] for o in outs])
    return (xp.reshape(bp, lp, D_MODEL), xs.reshape(bs, ls, D_MODEL),
            stack(outs_p, 0), stack(outs_p, 1), stack(outs_p, 2), stack(outs_p, 3),
            stack(outs_s, 0), stack(outs_s, 1), stack(outs_s, 2), stack(outs_s, 3))
```

```python
import functools

import jax
import jax.numpy as jnp
from jax import lax
from jax.experimental import pallas as pl
from jax.experimental.pallas import tpu as pltpu

F32 = jnp.float32
BF16 = jnp.bfloat16

D_MODEL = 2048
DEPTH = 4
GLA_HEADS = 4
GLA_HEAD_K = 256
GLA_HEAD_V = 512
GLA_DK = GLA_HEADS * GLA_HEAD_K
GLA_DV = GLA_HEADS * GLA_HEAD_V
GLA_LOWRANK = 16
GLA_GATE_NORMALIZER = 16.0
GLA_CHUNK = 64
GLA_SUB = 16
FOX_HEADS = 32
FOX_HEAD_DIM = 64
FOX_DIM = FOX_HEADS * FOX_HEAD_DIM
D_FF = 4 * D_MODEL
EPS = 1e-6

LANES = 128
MXU_DIM = 256
VMEM_LIMIT_BYTES = 56 * 1024 * 1024
PROJ_SLAB = 512

CE_PARTS = 3
CE_ONES = CE_PARTS * FOX_HEADS
LOWRANK_LANE = CE_ONES

FOX_QUERY_TILE = MXU_DIM
FOX_QUERY_BLOCK = 1024
FOX_KEY_BLOCK = 512
FOX_SAMPLE_PAIRS = 4
FOX_SUM_ROWS = 16
LOG2E = 1.4426950408889634
FOX_UNDERFLOW_LOG2 = -160.0
FOX_LOOP_UNROLL = 2
FOX_LOOP_TRIP = 4

NT_DIMS = (((1,), (1,)), ((), ()))
TN_DIMS = (((0,), (0,)), ((), ()))


def _params(*semantics):
    return pltpu.CompilerParams(dimension_semantics=semantics, vmem_limit_bytes=VMEM_LIMIT_BYTES)


def _log_sigmoid(x):
    return jnp.minimum(x, 0.0) - jnp.log(1.0 + jnp.exp(-jnp.abs(x)))


def _split3(x):
    p1 = x.astype(BF16)
    r1 = x - p1.astype(F32)
    p2 = r1.astype(BF16)
    p3 = (r1 - p2.astype(F32)).astype(BF16)
    return p1, p2, p3


def _lower_tri(n):
    r = lax.broadcasted_iota(jnp.int32, (n, n), 0)
    c = lax.broadcasted_iota(jnp.int32, (n, n), 1)
    return jnp.where(r >= c, 1.0, 0.0).astype(BF16)


def _cumsum_rows(tri, x):
    p1, p2, p3 = _split3(x)
    return (jnp.dot(tri, p1, preferred_element_type=F32)
            + jnp.dot(tri, p2, preferred_element_type=F32)
            + jnp.dot(tri, p3, preferred_element_type=F32))


def _rmsnorm_to_bf16(x_ref, g_ref, h_ref, rows=128):
    def body(r, carry):
        sl = pl.ds(pl.multiple_of(r * rows, rows), rows)
        x = x_ref[sl, :]
        ms = jnp.mean(x * x, axis=-1, keepdims=True)
        h_ref[sl, :] = (x * lax.rsqrt(ms + EPS) * g_ref[...]).astype(BF16)
        return carry
    lax.fori_loop(0, x_ref.shape[0] // rows, body, 0)


def _head_rms(acc, gsum_ref, gain):
    sq = acc * acc
    hi = sq.astype(BF16)
    lo = (sq - hi.astype(F32)).astype(BF16)
    parts = []
    for s in range(acc.shape[1] // MXU_DIM):
        sl = slice(s * MXU_DIM, (s + 1) * MXU_DIM)
        parts.append(jnp.dot(hi[:, sl], gsum_ref[...], preferred_element_type=F32)
                     + jnp.dot(lo[:, sl], gsum_ref[...], preferred_element_type=F32))
    ms = jnp.concatenate(parts, axis=1) * (1.0 / FOX_HEAD_DIM)
    return acc * lax.rsqrt(ms + EPS) * gain


def _proj_kernel(x_ref, g_ref, w_ref, *refs, mode):
    h_ref = refs[-1]

    @pl.when(pl.program_id(1) == 0)
    def _():
        _rmsnorm_to_bf16(x_ref, g_ref, h_ref)

    h = h_ref[...]
    slab = min(w_ref.shape[1], PROJ_SLAB)
    for s in range(w_ref.shape[1] // slab):
        cols = slice(s * slab, (s + 1) * slab)
        acc = jnp.dot(h, w_ref[:, cols], preferred_element_type=F32)
        if mode == "plain":
            (o_ref,) = refs[:-1]
            o_ref[:, cols] = acc.astype(o_ref.dtype)
        elif mode == "small":
            b_ref, o_ref = refs[:-1]
            lane = lax.broadcasted_iota(jnp.int32, acc.shape, 1)
            o_ref[:, cols] = jnp.where(lane < CE_ONES, _log_sigmoid(acc + b_ref[:, cols]), acc)
        elif mode == "qnorm":
            gsum_ref, gain_ref, o_ref = refs[:-1]
            y = _head_rms(acc, gsum_ref, gain_ref[:, cols])
            o_ref[:, cols] = (y * (FOX_HEAD_DIM ** -0.5 * LOG2E)).astype(BF16)
        elif mode == "knorm":
            gsum_ref, gain_ref, o32_ref, o16_ref = refs[:-1]
            y = _head_rms(acc, gsum_ref, gain_ref[:, cols])
            o32_ref[:, cols] = y
            o16_ref[:, cols] = y.astype(BF16)
        elif mode == "dual":
            o32_ref, o16_ref = refs[:-1]
            o32_ref[:, cols] = acc
            o16_ref[:, cols] = acc.astype(BF16)
        else:
            o32_ref, o16t_ref = refs[:-1]
            o32_ref[:, cols] = acc
            o16t_ref[cols, :] = acc.T.astype(BF16)


def _proj(x, g, w, mode, extras=()):
    m, n = x.shape[0], w.shape[1]
    tm = min(m, 1024)
    tn = min(n, 2 * PROJ_SLAB)
    grid = (m // tm, n // tn)
    in_specs = [
        pl.BlockSpec((tm, D_MODEL), lambda i, j: (i, 0)),
        pl.BlockSpec((1, D_MODEL), lambda i, j: (0, 0)),
        pl.BlockSpec((D_MODEL, tn), lambda i, j: (0, j)),
    ]
    tile = pl.BlockSpec((tm, tn), lambda i, j: (i, j))
    row = pl.BlockSpec((1, tn), lambda i, j: (0, j))
    if mode == "plain":
        out_shape, out_specs = jax.ShapeDtypeStruct((m, n), BF16), tile
    elif mode == "small":
        in_specs += [row]
        out_shape, out_specs = jax.ShapeDtypeStruct((m, n), F32), tile
    elif mode == "qnorm":
        in_specs += [pl.BlockSpec((MXU_DIM, MXU_DIM), lambda i, j: (0, 0)), row]
        out_shape, out_specs = jax.ShapeDtypeStruct((m, n), BF16), tile
    elif mode == "dual_t":
        out_shape = (jax.ShapeDtypeStruct((m, n), F32), jax.ShapeDtypeStruct((n, m), BF16))
        out_specs = (tile, pl.BlockSpec((tn, tm), lambda i, j: (j, i)))
    else:
        if mode == "knorm":
            in_specs += [pl.BlockSpec((MXU_DIM, MXU_DIM), lambda i, j: (0, 0)), row]
        out_shape = (jax.ShapeDtypeStruct((m, n), F32), jax.ShapeDtypeStruct((m, n), BF16))
        out_specs = (tile, tile)
    return pl.pallas_call(
        functools.partial(_proj_kernel, mode=mode),
        grid=grid, in_specs=in_specs, out_specs=out_specs, out_shape=out_shape,
        scratch_shapes=[pltpu.VMEM((tm, D_MODEL), BF16)],
        compiler_params=_params("parallel", "arbitrary"),
        name="proj_" + mode,
    )(x, g, w, *extras)


def _encode_cumsum(cs):
    c1, c2, c3 = _split3(cs * LOG2E)
    lane = lax.broadcasted_iota(jnp.int32, cs.shape, 1)
    one = jnp.ones_like(c1)
    return jnp.where(lane < FOX_HEADS, c1,
                     jnp.where(lane < 2 * FOX_HEADS, c2, jnp.where(lane < CE_ONES, c3, one)))


def _ce_prompt_kernel(s_ref, ce_ref, carry_ref):
    @pl.when(pl.program_id(0) == 0)
    def _():
        carry_ref[...] = jnp.zeros_like(carry_ref)

    t = s_ref.shape[0]
    cs = _cumsum_rows(_lower_tri(t), s_ref[...]) + carry_ref[...]
    carry_ref[...] = cs[t - 1:t, :]
    ce_ref[...] = _encode_cumsum(cs)


def _ce_prompt(small):
    m = small.shape[0]
    t = MXU_DIM
    return pl.pallas_call(
        _ce_prompt_kernel,
        grid=(m // t,),
        in_specs=[pl.BlockSpec((t, LANES), lambda i: (i, 0))],
        out_specs=pl.BlockSpec((t, LANES), lambda i: (i, 0)),
        out_shape=jax.ShapeDtypeStruct((m, LANES), BF16),
        scratch_shapes=[pltpu.VMEM((1, LANES), F32)],
        compiler_params=_params("arbitrary"),
        name="ce_prompt",
    )(small)


def _ce_sample_kernel(c_ref, s_ref, cec_ref, cen_ref):
    t = MXU_DIM
    tri = _lower_tri(t)
    carry = jnp.zeros((1, LANES), F32)
    for b in range(c_ref.shape[1] // t):
        cs = _cumsum_rows(tri, c_ref[0, b * t:(b + 1) * t, :]) + carry
        carry = cs[t - 1:t, :]
        cec_ref[0, b * t:(b + 1) * t, :] = _encode_cumsum(cs)
    n = s_ref.shape[0]
    cs = _cumsum_rows(_lower_tri(n), s_ref[...]) + carry
    cen_ref[...] = _encode_cumsum(cs)


def _ce_sample(cache_rep, small, batch, seq):
    past = cache_rep.shape[1]
    return pl.pallas_call(
        _ce_sample_kernel,
        grid=(batch,),
        in_specs=[pl.BlockSpec((1, past, LANES), lambda b: (b, 0, 0)),
                  pl.BlockSpec((seq, LANES), lambda b: (b, 0))],
        out_specs=(pl.BlockSpec((1, past, LANES), lambda b: (b, 0, 0)),
                   pl.BlockSpec((seq, LANES), lambda b: (b, 0))),
        out_shape=(jax.ShapeDtypeStruct((batch, past, LANES), BF16),
                   jax.ShapeDtypeStruct((batch * seq, LANES), BF16)),
        compiler_params=_params("parallel"),
        name="ce_sample",
    )(cache_rep, small)


def _gla_kernel(*refs, chunk, has_state):
    if has_state:
        (q_ref, k_ref, v_ref, gg_ref, sm_ref, au_ref, ab_ref, gn_ref, s0_ref,
         o_ref, sout_ref, st_ref) = refs
    else:
        (q_ref, k_ref, v_ref, gg_ref, sm_ref, au_ref, ab_ref, gn_ref,
         o_ref, sout_ref, st_ref) = refs
    blk = pl.program_id(1)
    hk, hv = GLA_HEAD_K, GLA_HEAD_V
    tb = q_ref.shape[0]
    n_chunks, n_sub = tb // chunk, chunk // GLA_SUB
    heads = range(GLA_HEADS)

    @pl.when(blk == 0)
    def _():
        for h in heads:
            st_ref[h] = s0_ref[0, h].T if has_state else jnp.zeros((hv, hk), F32)

    a = jnp.dot(sm_ref[...].astype(BF16), au_ref[...], preferred_element_type=F32) + ab_ref[...]
    log_a = _log_sigmoid(a) * (1.0 / GLA_GATE_NORMALIZER)
    r = lax.broadcasted_iota(jnp.int32, (tb, tb), 0)
    c = lax.broadcasted_iota(jnp.int32, (tb, tb), 1)
    tri = jnp.where((r >= c) & (r // chunk == c // chunk), 1.0, 0.0).astype(BF16)
    bc = _cumsum_rows(tri, log_a)

    def rows_like(row_of, group):
        return jnp.concatenate([jnp.broadcast_to(bc[row_of(g):row_of(g) + 1, :], (group, bc.shape[1]))
                                for g in range(tb // group)], axis=0)

    last = rows_like(lambda g: (g + 1) * chunk - 1, chunk)
    anchor = rows_like(lambda g: g * GLA_SUB, GLA_SUB)
    scale = hk ** -0.5
    q = q_ref[...].astype(F32)
    k = k_ref[...].astype(F32)
    q_inter = (q * jnp.exp(bc) * scale).astype(BF16)
    k_dec = (k * jnp.exp(last - bc)).astype(BF16)
    q_sub = (q * jnp.exp(bc - anchor) * scale).astype(BF16)

    att = {}
    for ch in range(n_chunks):
        r0 = ch * chunk
        q_slots, k_slots = [], []
        for i in range(n_sub):
            lo, hi = i * GLA_SUB, (i + 1) * GLA_SUB
            k_i = (k[r0:r0 + hi] * jnp.exp(anchor[r0 + lo:r0 + lo + 1] - bc[r0:r0 + hi])).astype(BF16)
            k_slots.append(jnp.concatenate([k_i, jnp.zeros((chunk - hi, k.shape[1]), BF16)], axis=0)
                           if hi < chunk else k_i)
            pieces = [jnp.zeros((lo, q.shape[1]), BF16)] if lo else []
            pieces.append(q_sub[r0 + lo:r0 + hi])
            if hi < chunk:
                pieces.append(jnp.zeros((chunk - hi, q.shape[1]), BF16))
            q_slots.append(jnp.concatenate(pieces, axis=0) if len(pieces) > 1 else pieces[0])
        for h in heads:
            kc = slice(h * hk, (h + 1) * hk)
            q_hat = jnp.concatenate([s[:, kc] for s in q_slots], axis=1)
            k_hat = jnp.concatenate([s[:, kc] for s in k_slots], axis=1)
            att[ch, h] = lax.dot_general(q_hat, k_hat, NT_DIMS, preferred_element_type=F32)
    ri = lax.broadcasted_iota(jnp.int32, (chunk, chunk), 0)
    ci = lax.broadcasted_iota(jnp.int32, (chunk, chunk), 1)
    o_intra = {}
    for (ch, h), s in att.items():
        p = jnp.where(ci <= ri, s, 0.0).astype(BF16)
        o_intra[ch, h] = jnp.dot(p, v_ref[ch * chunk:(ch + 1) * chunk, h * hv:(h + 1) * hv],
                                 preferred_element_type=F32)

    for ch in range(n_chunks):
        rows = slice(ch * chunk, (ch + 1) * chunk)
        decay = jnp.exp(bc[(ch + 1) * chunk - 1:(ch + 1) * chunk, :])
        for h in heads:
            kc = slice(h * hk, (h + 1) * hk)
            vc = slice(h * hv, (h + 1) * hv)
            st = st_ref[h]
            o = o_intra[ch, h] + lax.dot_general(q_inter[rows, kc], st.astype(BF16), NT_DIMS,
                                                 preferred_element_type=F32)
            st_ref[h] = st * decay[:, kc] + lax.dot_general(v_ref[rows, vc], k_dec[rows, kc], TN_DIMS,
                                                            preferred_element_type=F32)
            ms = jnp.mean(o * o, axis=-1, keepdims=True)
            y = o * lax.rsqrt(ms + EPS) * gn_ref[...]
            gg = gg_ref[rows, vc].astype(F32)
            o_ref[rows, vc] = (y * (gg * jax.nn.sigmoid(gg))).astype(BF16)

    @pl.when(blk == pl.num_programs(1) - 1)
    def _():
        for h in heads:
            sout_ref[0, h] = st_ref[h].T


def _gla(pg, small, alpha_pad, alpha_b, gnorm, s0, batch, seq):
    chunk = min(GLA_CHUNK, seq)
    tb = min(seq, 4 * chunk)
    nblk = seq // tb
    hk, hv = GLA_HEAD_K, GLA_HEAD_V
    row = lambda b, t: b * nblk + t
    in_specs = [
        pl.BlockSpec((tb, GLA_DK), lambda b, t: (row(b, t), 0)),
        pl.BlockSpec((tb, GLA_DK), lambda b, t: (row(b, t), 1)),
        pl.BlockSpec((tb, GLA_DV), lambda b, t: (row(b, t), 1)),
        pl.BlockSpec((tb, GLA_DV), lambda b, t: (row(b, t), 2)),
        pl.BlockSpec((tb, LANES), lambda b, t: (row(b, t), 0)),
        pl.BlockSpec((LANES, GLA_DK), lambda b, t: (0, 0)),
        pl.BlockSpec((1, GLA_DK), lambda b, t: (0, 0)),
        pl.BlockSpec((1, hv), lambda b, t: (0, 0)),
    ]
    args = [pg, pg, pg, pg, small, alpha_pad, alpha_b, gnorm]
    state_spec = pl.BlockSpec((1, GLA_HEADS, hk, hv), lambda b, t: (b, 0, 0, 0))
    if s0 is not None:
        in_specs.append(state_spec)
        args.append(s0)
    return pl.pallas_call(
        functools.partial(_gla_kernel, chunk=chunk, has_state=s0 is not None),
        grid=(batch, nblk),
        in_specs=in_specs,
        out_specs=(pl.BlockSpec((tb, GLA_DV), lambda b, t: (row(b, t), 0)), state_spec),
        out_shape=(jax.ShapeDtypeStruct((batch * seq, GLA_DV), BF16),
                   jax.ShapeDtypeStruct((batch, GLA_HEADS, hk, hv), F32)),
        scratch_shapes=[pltpu.VMEM((GLA_HEADS, hv, hk), F32)],
        compiler_params=_params("parallel", "arbitrary"),
        name="gla",
    )(*args)


def _fox_query_operand(q, ce_q, head, half):
    lane = lax.broadcasted_iota(jnp.int32, (1, LANES), 1)
    keep = jnp.where((lane >= half * FOX_HEAD_DIM) & (lane < (half + 1) * FOX_HEAD_DIM), 1.0, 0.0)
    q_m = q * keep.astype(BF16)
    r = lax.broadcasted_iota(jnp.int32, (LANES, LANES), 0)
    c = lax.broadcasted_iota(jnp.int32, (LANES, LANES), 1)
    pick = jnp.zeros((LANES, LANES), F32)
    for part in range(CE_PARTS):
        pick = jnp.where((r == head + part * FOX_HEADS) & (c == CE_ONES + part), 1.0, pick)
    enc = jnp.dot(ce_q, pick.astype(BF16), preferred_element_type=F32)
    minus = jnp.zeros((1, LANES), F32)
    for part in range(CE_PARTS):
        minus = jnp.where(lane == head + part * FOX_HEADS, -1.0, minus)
    return jnp.concatenate([q_m, (enc + minus).astype(BF16)], axis=1)


def _fox_prompt_kernel(first_ref, q_ref, ceq_ref, k_ref, cek_ref, vt_ref, o_ref, st_ref, *, tq, tk):
    pair = pl.program_id(0)
    qi = pl.program_id(1)
    first = [first_ref[(2 * pair + half) * pl.num_programs(1) + qi] for half in range(2)]
    qt = FOX_QUERY_TILE
    nt = tq // qt
    tail_blocks = tq // tk
    full_blocks = qi * tail_blocks
    chains = [(half, r) for half in range(2) for r in range(nt)]
    everyone = list(range(len(chains)))
    needs = lambda t: [n for n in everyone if t < tail_blocks and (chains[n][1] + 1) * qt > t * tk]
    acc_rows = FOX_HEAD_DIM + FOX_SUM_ROWS

    def query_operands():
        ops = [_fox_query_operand(q_ref[r * qt:(r + 1) * qt, :], ceq_ref[r * qt:(r + 1) * qt, :],
                                  2 * pair + half, half) for half, r in chains]
        return [op.astype(F32).T.astype(BF16) for op in ops]

    def key_operand(block):
        ks = pl.multiple_of(block * tk, tk)
        return jnp.concatenate([k_ref[pl.ds(ks, tk), :], cek_ref[pl.ds(ks, tk), :]], axis=1)

    def value_operands(block):
        ks = pl.multiple_of(block * tk, tk)
        ones = jnp.ones((FOX_SUM_ROWS, tk), BF16)
        return [jnp.concatenate([vt_ref[h * FOX_HEAD_DIM:(h + 1) * FOX_HEAD_DIM, pl.ds(ks, tk)], ones], axis=0)
                for h in range(2)]

    def causal(st, r, tail_block):
        key = lax.broadcasted_iota(jnp.int32, st.shape, 0) + (tail_block * tk - r * qt)
        qry = lax.broadcasted_iota(jnp.int32, st.shape, 1)
        return jnp.where(key <= qry, st, -jnp.inf)

    def write_out(accs):
        for r in range(nt):
            outs = []
            for half in range(2):
                acc = accs[chains.index((half, r))]
                outs.append(acc[:FOX_HEAD_DIM] / acc[FOX_HEAD_DIM:FOX_HEAD_DIM + 1])
            o_ref[r * qt:(r + 1) * qt, :] = jnp.concatenate(outs, axis=0).T.astype(BF16)

    q_ops = query_operands()

    def step(block, slot, carries, attend, score_next, tail_block=None):
        carries = list(carries)
        if attend:
            vts = value_operands(block)
        if score_next:
            k_op = key_operand(block + 1)
        for n in everyone:
            if n in score_next:
                st_ref[1 - slot, n] = jnp.dot(k_op, q_ops[n], preferred_element_type=F32)
            if n in attend:
                half, r = chains[n]
                m, acc = carries[n]
                st = st_ref[slot, n]
                if tail_block is not None:
                    st = causal(st, r, tail_block)
                m_new = jnp.maximum(m, jnp.max(st, axis=0, keepdims=True))
                p = jnp.exp2(st - m_new).astype(BF16)
                acc = jnp.exp2(m - m_new) * acc + jnp.dot(vts[half], p, preferred_element_type=F32)
                carries[n] = (m_new, acc)
        return tuple(carries)

    def trip(which, blocks, start, carries):
        for u in range(blocks):
            carries = step(start + u, u % 2, carries, which, which)
        return carries

    def run(which, start, stop, carries):
        short = ((stop - start) // FOX_LOOP_UNROLL) % (FOX_LOOP_TRIP // FOX_LOOP_UNROLL)
        carries = lax.fori_loop(0, short, lambda i, c: trip(which, FOX_LOOP_UNROLL, start, c), carries)
        start = start + FOX_LOOP_UNROLL * short
        return lax.fori_loop(0, (stop - start) // FOX_LOOP_TRIP,
                             lambda i, c: trip(which, FOX_LOOP_TRIP, start + FOX_LOOP_TRIP * i, c), carries)

    carries = tuple((jnp.full((1, qt), -jnp.inf, F32), jnp.zeros((acc_rows, qt), F32)) for _ in chains)
    together = jnp.maximum(first[0], first[1])
    for half in range(2):
        mine = [n for n in everyone if chains[n][0] == half]
        carries = step(first[half] - 1, 1, carries, [], mine)
        carries = run(mine, first[half], together, carries)
    carries = run(everyone, together, full_blocks, carries)
    for t in range(tail_blocks):
        carries = step(full_blocks + t, t % 2, carries, needs(t), needs(t + 1), tail_block=t)
    write_out([acc for _, acc in carries])


def _fox_first_blocks(logf, gq, gk, tq, tk):
    nq, nk, ratio = logf.shape[0] // tq, logf.shape[0] // tk, tq // tk
    bound = FOX_HEAD_DIM ** 0.5 * jnp.max(jnp.abs(gq)) * jnp.max(jnp.abs(gk)) * LOG2E * 1.02 + 1.0
    block_sums = jnp.sum(logf.reshape(nk, tk, FOX_HEADS), axis=1)
    c_k = jnp.dot(jnp.tril(jnp.ones((nk, nk), F32)), block_sums, precision=lax.Precision.HIGHEST) * LOG2E
    c_q = jnp.concatenate([jnp.zeros((1, FOX_HEADS), F32), c_k[ratio - 1::ratio][:nq - 1]], axis=0)
    key_block = jnp.arange(nk)[None, :, None]
    dead = 2.0 * bound + (c_q[:, None, :] - c_k[None, :, :]) < FOX_UNDERFLOW_LOG2
    dead &= key_block < (jnp.arange(nq) * ratio)[:, None, None]
    first = jnp.min(jnp.where(dead, nk, key_block), axis=1)
    first = first // FOX_LOOP_UNROLL * FOX_LOOP_UNROLL
    return first.T.reshape(-1).astype(jnp.int32)


def _fox_prompt(q16, k16, vt16, ce, logf, gq, gk):
    m = q16.shape[0]
    tq = min(m, FOX_QUERY_BLOCK)
    tk = min(tq // 2, FOX_KEY_BLOCK)
    assert (tq // tk) % FOX_LOOP_UNROLL == 0 and tk % FOX_QUERY_TILE == 0
    pairs = FOX_HEADS // 2
    chains = 2 * (tq // FOX_QUERY_TILE)
    return pl.pallas_call(
        functools.partial(_fox_prompt_kernel, tq=tq, tk=tk),
        grid=(pairs, m // tq),
        in_specs=[
            pl.BlockSpec(memory_space=pltpu.SMEM),
            pl.BlockSpec((tq, LANES), lambda p, i: (i, p)),
            pl.BlockSpec((tq, LANES), lambda p, i: (i, 0)),
            pl.BlockSpec((m, LANES), lambda p, i: (0, p)),
            pl.BlockSpec((m, LANES), lambda p, i: (0, 0)),
            pl.BlockSpec((LANES, m), lambda p, i: (p, 0)),
        ],
        out_specs=pl.BlockSpec((tq, LANES), lambda p, i: (i, p)),
        out_shape=jax.ShapeDtypeStruct((m, FOX_DIM), BF16),
        scratch_shapes=[pltpu.VMEM((2, chains, tk, FOX_QUERY_TILE), F32)],
        compiler_params=_params("parallel", "arbitrary"),
        name="fox_prompt",
    )(_fox_first_blocks(logf, gq, gk, tq, tk), q16, ce, k16, ce, vt16)


def _fox_sample_kernel(q_ref, kn_ref, vn_ref, cen_ref, kc_ref, vc_ref, cec_ref, o_ref, k_all, v_all):
    group = pl.program_id(1)
    past = kc_ref.shape[0]
    seq = q_ref.shape[0]
    total = k_all.shape[1]
    pairs = range(FOX_SAMPLE_PAIRS)
    for u in pairs:
        lanes = slice(u * LANES, (u + 1) * LANES)
        k_all[u, 0:past, 0:LANES] = kc_ref[:, lanes].astype(BF16)
        k_all[u, 0:past, LANES:2 * LANES] = cec_ref[0]
        k_all[u, past:past + seq, 0:LANES] = kn_ref[:, lanes]
        k_all[u, past:past + seq, LANES:2 * LANES] = cen_ref[...]
        k_all[u, past + seq:total, :] = jnp.zeros((total - past - seq, 2 * LANES), BF16)
        v_all[u, 0:past, :] = vc_ref[:, lanes].astype(BF16)
        v_all[u, past:past + seq, :] = vn_ref[:, lanes]
        v_all[u, past + seq:total, :] = jnp.zeros((total - past - seq, LANES), BF16)

    q_ops = [jnp.concatenate([_fox_query_operand(q_ref[:, u * LANES:(u + 1) * LANES], cen_ref[...],
                                                 2 * (FOX_SAMPLE_PAIRS * group + u) + half, half)
                              for half in range(2)], axis=0) for u in pairs]
    scores = [lax.dot_general(q_ops[u], k_all[u], NT_DIMS, preferred_element_type=F32) for u in pairs]
    qr = lax.broadcasted_iota(jnp.int32, scores[0].shape, 0)
    qr = jnp.where(qr >= seq, qr - seq, qr)
    kc = lax.broadcasted_iota(jnp.int32, scores[0].shape, 1)
    probs, sums = [], []
    for s in scores:
        s = jnp.where(kc <= qr + past, s, -jnp.inf)
        p = jnp.exp2(s - jnp.max(s, axis=1, keepdims=True))
        sums.append(jnp.sum(p, axis=1, keepdims=True))
        probs.append(p.astype(BF16))
    lane = lax.broadcasted_iota(jnp.int32, (seq, LANES), 1)
    for u in pairs:
        o = jnp.dot(probs[u], v_all[u], preferred_element_type=F32) / sums[u]
        o_ref[:, u * LANES:(u + 1) * LANES] = jnp.where(lane < FOX_HEAD_DIM, o[:seq], o[seq:]).astype(BF16)


def _fox_sample(q16, k16, v16, ce_new, cache_k, cache_v, layer, ce_cache, batch, seq):
    past = cache_k.shape[2]
    total = -(-(past + seq) // LANES) * LANES
    width = FOX_SAMPLE_PAIRS * LANES
    groups = FOX_DIM // width
    new = pl.BlockSpec((seq, width), lambda b, p: (b, p))
    cache = pl.BlockSpec((None, None, past, width), lambda b, p: (layer, b, 0, p))
    return pl.pallas_call(
        _fox_sample_kernel,
        grid=(batch, groups),
        in_specs=[new, new, new, pl.BlockSpec((seq, LANES), lambda b, p: (b, 0)),
                  cache, cache, pl.BlockSpec((1, past, LANES), lambda b, p: (b, 0, 0))],
        out_specs=new,
        out_shape=jax.ShapeDtypeStruct((batch * seq, FOX_DIM), BF16),
        scratch_shapes=[pltpu.VMEM((FOX_SAMPLE_PAIRS, total, 2 * LANES), BF16),
                        pltpu.VMEM((FOX_SAMPLE_PAIRS, total, LANES), BF16)],
        compiler_params=_params("parallel", "arbitrary"),
        name="fox_sample",
    )(q16, k16, v16, ce_new, cache_k, cache_v, ce_cache)


def _merge_kernel(oa_ref, wa_ref, ob_ref, wb_ref, ga_ref, gb_ref, o_ref):
    a = jnp.dot(oa_ref[...], wa_ref[...], preferred_element_type=F32)
    b = jnp.dot(ob_ref[...], wb_ref[...], preferred_element_type=F32)
    ga = jax.nn.sigmoid(ga_ref[...].astype(F32))
    gb = jax.nn.sigmoid(gb_ref[...].astype(F32))
    o_ref[...] = (ga * a + gb * b).astype(BF16)


def _merge(o_gla, w_a, o_fox, w_b, gates):
    m = o_gla.shape[0]
    tm, tn = min(m, 1024), 1024
    nb = D_MODEL // tn
    act = pl.BlockSpec((tm, D_MODEL), lambda i, j: (i, 0))
    wgt = pl.BlockSpec((D_MODEL, tn), lambda i, j: (0, j))
    return pl.pallas_call(
        _merge_kernel,
        grid=(m // tm, nb),
        in_specs=[act, wgt, act, wgt,
                  pl.BlockSpec((tm, tn), lambda i, j: (i, j)),
                  pl.BlockSpec((tm, tn), lambda i, j: (i, j + nb))],
        out_specs=pl.BlockSpec((tm, tn), lambda i, j: (i, j)),
        out_shape=jax.ShapeDtypeStruct((m, D_MODEL), BF16),
        compiler_params=_params("parallel", "arbitrary"),
        name="merge",
    )(o_gla, w_a, o_fox, w_b, gates, gates)


def _out_proj_kernel(a_ref, w_ref, x_ref, o_ref):
    o_ref[...] = x_ref[...] + jnp.dot(a_ref[...], w_ref[...], preferred_element_type=F32)


def _out_proj(merged, w, x):
    m = x.shape[0]
    tm, tn = min(m, 1024), 1024
    return pl.pallas_call(
        _out_proj_kernel,
        grid=(m // tm, D_MODEL // tn),
        in_specs=[pl.BlockSpec((tm, D_MODEL), lambda i, j: (i, 0)),
                  pl.BlockSpec((D_MODEL, tn), lambda i, j: (0, j)),
                  pl.BlockSpec((tm, tn), lambda i, j: (i, j))],
        out_specs=pl.BlockSpec((tm, tn), lambda i, j: (i, j)),
        out_shape=jax.ShapeDtypeStruct((m, D_MODEL), F32),
        compiler_params=_params("parallel", "arbitrary"),
        name="out_proj",
    )(merged, w, x)


def _mlp_kernel(x_ref, g_ref, wu_ref, wd_ref, o_ref, h_ref):
    @pl.when(pl.program_id(1) == 0)
    def _():
        _rmsnorm_to_bf16(x_ref, g_ref, h_ref)
        o_ref[...] = x_ref[...]

    u = jnp.maximum(jnp.dot(h_ref[...], wu_ref[...], preferred_element_type=F32), 0.0)
    o_ref[...] += jnp.dot((u * u).astype(BF16), wd_ref[...], preferred_element_type=F32)


def _mlp(x, g, w_up, w_down):
    m = x.shape[0]
    tm, tf = min(m, 1024), 512
    return pl.pallas_call(
        _mlp_kernel,
        grid=(m // tm, D_FF // tf),
        in_specs=[pl.BlockSpec((tm, D_MODEL), lambda i, f: (i, 0)),
                  pl.BlockSpec((1, D_MODEL), lambda i, f: (0, 0)),
                  pl.BlockSpec((D_MODEL, tf), lambda i, f: (0, f)),
                  pl.BlockSpec((tf, D_MODEL), lambda i, f: (f, 0))],
        out_specs=pl.BlockSpec((tm, D_MODEL), lambda i, f: (i, 0)),
        out_shape=jax.ShapeDtypeStruct((m, D_MODEL), F32),
        scratch_shapes=[pltpu.VMEM((tm, D_MODEL), BF16)],
        compiler_params=_params("parallel", "arbitrary"),
        name="mlp",
    )(x, g, w_up, w_down)


def _prepare_layer(l, norm_mix_g, w_in, gla_alpha_up, gla_alpha_b, gla_norm_g, fox_f_b, fox_qnorm_g,
                   fox_knorm_g, w_br_gla, w_br_fox, w_out, norm_mlp_g, w_up, w_down):
    w = w_in[l]
    o = 0
    cols = {}
    for name, size in (("gla", 2 * GLA_DK + 2 * GLA_DV), ("low", GLA_LOWRANK), ("fq", FOX_DIM),
                       ("fk", FOX_DIM), ("fv", FOX_DIM), ("ff", FOX_HEADS), ("gate", 2 * D_MODEL)):
        cols[name] = w[:, o:o + size]
        o += size
    pad = LANES - CE_ONES - GLA_LOWRANK
    w_small = jnp.concatenate([cols["ff"]] * CE_PARTS + [cols["low"], jnp.zeros((D_MODEL, pad), F32)], axis=1)
    b_small = jnp.concatenate([fox_f_b[l]] * CE_PARTS + [jnp.zeros((LANES - CE_ONES,), F32)])[None, :]
    alpha_pad = jnp.zeros((LANES, GLA_DK), F32).at[LOWRANK_LANE:LOWRANK_LANE + GLA_LOWRANK].set(gla_alpha_up[l])
    return dict(
        norm_mix_g=norm_mix_g[l][None, :], norm_mlp_g=norm_mlp_g[l][None, :],
        w_gla=cols["gla"].astype(BF16), w_small=w_small.astype(BF16), b_small=b_small,
        w_fq=cols["fq"].astype(BF16), w_fk=cols["fk"].astype(BF16), w_fv=cols["fv"].astype(BF16),
        w_gate=cols["gate"].astype(BF16),
        alpha_pad=alpha_pad.astype(BF16), alpha_b=gla_alpha_b[l][None, :], gla_norm_g=gla_norm_g[l][None, :],
        gq=jnp.tile(fox_qnorm_g[l], FOX_HEADS)[None, :], gk=jnp.tile(fox_knorm_g[l], FOX_HEADS)[None, :],        w_br_gla=w_br_gla[l].astype(BF16), w_br_fox=w_br_fox[l].astype(BF16), w_out=w_out[l].astype(BF16),
        w_up=w_up[l].astype(BF16), w_down=w_down[l].astype(BF16),
    )


def _group_sum_matrix():
    r = lax.broadcasted_iota(jnp.int32, (MXU_DIM, MXU_DIM), 0) // FOX_HEAD_DIM
    c = lax.broadcasted_iota(jnp.int32, (MXU_DIM, MXU_DIM), 1) // FOX_HEAD_DIM
    return (r == c).astype(BF16)


def _layer(x, batch, seq, cache, p, gsum):
    g = p["norm_mix_g"]
    pg = _proj(x, g, p["w_gla"], "plain")
    small = _proj(x, g, p["w_small"], "small", (p["b_small"],))
    q16 = _proj(x, g, p["w_fq"], "qnorm", (gsum, p["gq"]))
    k32, k16 = _proj(x, g, p["w_fk"], "knorm", (gsum, p["gk"]))
    v32, v16 = _proj(x, g, p["w_fv"], "dual_t" if cache is None else "dual")
    gates = _proj(x, g, p["w_gate"], "plain")

    if cache is None:
        assert batch == 1
        o_gla, s_new = _gla(pg, small, p["alpha_pad"], p["alpha_b"], p["gla_norm_g"], None, batch, seq)
        ce = _ce_prompt(small)
        o_fox = _fox_prompt(q16, k16, v16, ce, small[:, :FOX_HEADS], p["gq"], p["gk"])
    else:
        cache_k, cache_v, cache_logf, s0, layer = cache
        depth, past = cache_k.shape[0], cache_k.shape[2]
        o_gla, s_new = _gla(pg, small, p["alpha_pad"], p["alpha_b"], p["gla_norm_g"], s0, batch, seq)
        rep = jnp.concatenate([cache_logf] * CE_PARTS + [jnp.zeros((batch, past, LANES - CE_ONES), F32)], axis=-1)
        ce_cache, ce_new = _ce_sample(rep, small, batch, seq)
        o_fox = _fox_sample(q16, k16, v16, ce_new, cache_k.reshape(depth, batch, past, FOX_DIM),
                            cache_v.reshape(depth, batch, past, FOX_DIM), layer, ce_cache, batch, seq)

    merged = _merge(o_gla, p["w_br_gla"], o_fox, p["w_br_fox"], gates)
    x = _out_proj(merged, p["w_out"], x)
    x = _mlp(x, p["norm_mlp_g"], p["w_up"], p["w_down"])
    k_out = k32.reshape(batch, seq, FOX_HEADS, FOX_HEAD_DIM)
    v_out = v32.reshape(batch, seq, FOX_HEADS, FOX_HEAD_DIM)
    logf = small[:, :FOX_HEADS].reshape(batch, seq, FOX_HEADS)
    return x, k_out, v_out, logf, s_new


def kernel(x_prompt, x_sample, cache_fox_k, cache_fox_v, cache_fox_logf, state_gla, norm_mix_g, w_in,
           gla_alpha_up, gla_alpha_b, gla_norm_g, fox_f_b, fox_qnorm_g, fox_knorm_g, w_br_gla, w_br_fox,
           w_out, norm_mlp_g, w_up, w_down):
    bp, lp, _ = x_prompt.shape
    bs, ls, _ = x_sample.shape
    xp = x_prompt.reshape(bp * lp, D_MODEL)
    xs = x_sample.reshape(bs * ls, D_MODEL)
    gsum = _group_sum_matrix()
    outs_p, outs_s = [], []
    for l in range(w_in.shape[0]):
        p = _prepare_layer(l, norm_mix_g, w_in, gla_alpha_up, gla_alpha_b, gla_norm_g, fox_f_b, fox_qnorm_g,
                           fox_knorm_g, w_br_gla, w_br_fox, w_out, norm_mlp_g, w_up, w_down)
        xp, *rest_p = _layer(xp, bp, lp, None, p, gsum)
        xs, *rest_s = _layer(xs, bs, ls, (cache_fox_k, cache_fox_v, cache_fox_logf[l], state_gla[l], l), p, gsum)
        outs_p.append(rest_p)
        outs_s.append(rest_s)
    stack = lambda outs, i: jnp.stack([o[i] for o in outs])
    return (xp.reshape(bp, lp, D_MODEL), xs.reshape(bs, ls, D_MODEL),
            stack(outs_p, 0), stack(outs_p, 1), stack(outs_p, 2), stack(outs_p, 3),
            stack(outs_s, 0), stack(outs_s, 1), stack(outs_s, 2), stack(outs_s, 3))
```
